```python
import jax, jax.numpy as jnp
from jax import lax
import numpy as np

D_MODEL = 2048
BATCH = 2
SEQ = 8192
DEPTH = 1

N_MEM = 256
GLA_HEADS = 4
GLA_KEY = D_MODEL // 2
GLA_VAL = D_MODEL
GLA_DK = GLA_KEY // GLA_HEADS
GLA_DV = GLA_VAL // GLA_HEADS
GLA_RANK = 16
GLA_TAU = 16.0
HGRN_DK = 128
HGRN_HEADS = D_MODEL // HGRN_DK
HGRN_KEY = HGRN_HEADS * HGRN_DK
HGRN_VAL = D_MODEL
HGRN_DV = HGRN_VAL // HGRN_HEADS
CHUNK = 64
X_HEADS = 4
X_DH = D_MODEL // X_HEADS
N_EXPERTS = 32
TOP_K = 4
D_EXPERT = D_MODEL
SWIGLU_LIMIT = 7.0
SWIGLU_ALPHA = 1.702
MOE_BLOCK = 256
DEEPNORM_ALPHA = (2.0 * DEPTH) ** 0.25
DEEPNORM_BETA = (8.0 * DEPTH) ** -0.25
NORM_EPS = 1e-5

IN_SPLITS = (GLA_KEY, GLA_KEY, GLA_VAL, GLA_VAL, GLA_RANK,
             HGRN_KEY, HGRN_KEY, HGRN_VAL, HGRN_VAL,
             D_MODEL, D_MODEL)
D_IN = sum(IN_SPLITS)
SPLIT_OFFSETS = tuple(int(o) for o in np.cumsum(IN_SPLITS)[:-1])

kernel_name = "hybrid_gla_hgrn2_deepnorm_memxattn_moe"


def layer_norm(x, g, b):
    xf = x.astype(jnp.float32)
    mu = jnp.mean(xf, axis=-1, keepdims=True)
    var = jnp.mean(jnp.square(xf - mu), axis=-1, keepdims=True)
    return ((xf - mu) * lax.rsqrt(var + NORM_EPS) * g + b).astype(x.dtype)


def head_rmsnorm(o, g):
    return o * lax.rsqrt(jnp.mean(jnp.square(o), axis=-1, keepdims=True) + NORM_EPS) * g.astype(jnp.float32)


def to_heads(t, n_heads):
    b, s, _ = t.shape
    return t.reshape(b, s, n_heads, -1).transpose(0, 2, 1, 3)


def from_heads(t):
    b, h, s, d = t.shape
    return t.transpose(0, 2, 1, 3).reshape(b, s, h * d)


def chunked_gated_linear_attention(q, k, v, log_g):
    b, h, s, dk = q.shape
    dv = v.shape[-1]
    n_chunks = s // CHUNK

    def to_chunks(t):
        return t.astype(jnp.float32).reshape(b, h, n_chunks, CHUNK, t.shape[-1]).transpose(2, 0, 1, 3, 4)

    causal = jnp.tril(jnp.ones((CHUNK, CHUNK), dtype=bool))[:, :, None]

    def step(state, inp):
        qc, kc, vc, gc = inp
        cum = jnp.cumsum(gc, axis=2)
        o_inter = jnp.einsum("bhcd,bhde->bhce", qc * jnp.exp(cum), state)
        diff = cum[:, :, :, None, :] - cum[:, :, None, :, :]
        decay = jnp.exp(jnp.where(causal, diff, -jnp.inf))
        scores = jnp.einsum("bhid,bhjd,bhijd->bhij", qc, kc, decay)
        o_intra = jnp.einsum("bhij,bhje->bhie", scores, vc)
        last = cum[:, :, -1:, :]
        new_state = (jnp.exp(last)[:, :, 0, :, None] * state
                     + jnp.einsum("bhcd,bhce->bhde", kc * jnp.exp(last - cum), vc))
        return new_state, o_inter + o_intra

    state0 = jnp.zeros((b, h, dk, dv), jnp.float32)
    _, o = lax.scan(step, state0, (to_chunks(q), to_chunks(k), to_chunks(v), to_chunks(log_g)))
    return o.transpose(1, 2, 0, 3, 4).reshape(b, h, s, dv)


def token_mixer(h, w_in, b_in, w_gla_a2, b_gla_a, gla_norm_g, hgrn_norm_g, hgrn_lb, w_o):
    dt = h.dtype
    proj = h @ w_in + b_in
    gq, gk, gv, gr, ga1, hq, hf, hi, hg, m_a, m_b = jnp.split(proj, SPLIT_OFFSETS, axis=-1)

    log_alpha = jax.nn.log_sigmoid((ga1 @ w_gla_a2 + b_gla_a).astype(jnp.float32)) / GLA_TAU
    o_gla = chunked_gated_linear_attention(
        to_heads(gq.astype(jnp.float32) * GLA_DK ** -0.5, GLA_HEADS), to_heads(gk, GLA_HEADS),
        to_heads(gv, GLA_HEADS), to_heads(log_alpha, GLA_HEADS))
    o_gla = from_heads(head_rmsnorm(o_gla, gla_norm_g)) * jax.nn.silu(gr.astype(jnp.float32))

    log_f = jnp.logaddexp(jnp.log(hgrn_lb), jnp.log1p(-hgrn_lb) + jax.nn.log_sigmoid(hf.astype(jnp.float32)))
    k_h = -jnp.expm1(log_f)
    o_h = chunked_gated_linear_attention(
        to_heads(jax.nn.silu(hq.astype(jnp.float32)), HGRN_HEADS), to_heads(k_h, HGRN_HEADS),
        to_heads(hi, HGRN_HEADS), to_heads(log_f, HGRN_HEADS))
    o_h = from_heads(head_rmsnorm(o_h, hgrn_norm_g)) * jax.nn.sigmoid(hg.astype(jnp.float32))

    merged = (jax.nn.sigmoid(m_a.astype(jnp.float32)) * o_gla
              + jax.nn.sigmoid(m_b.astype(jnp.float32)) * o_h)
    return merged.astype(dt) @ w_o


def memory_cross_attention(h, mem, w_xq, w_mem_kv, w_xo):
    b, s, d = h.shape
    q = (h @ w_xq).reshape(b, s, X_HEADS, X_DH)
    k, v = jnp.split(mem @ w_mem_kv, 2, axis=-1)
    k = k.reshape(b, N_MEM, X_HEADS, X_DH)
    v = v.reshape(b, N_MEM, X_HEADS, X_DH)
    scores = jnp.einsum("bshd,bmhd->bhsm", q, k).astype(jnp.float32) * X_DH ** -0.5
    p = jax.nn.softmax(scores, axis=-1)
    o = jnp.einsum("bhsm,bmhd->bshd", p.astype(v.dtype), v).reshape(b, s, d)
    return o @ w_xo


def clamped_swiglu(gu):
    glu = jnp.minimum(gu[..., ::2], SWIGLU_LIMIT)
    lin = jnp.clip(gu[..., 1::2], -SWIGLU_LIMIT, SWIGLU_LIMIT)
    return glu * jax.nn.sigmoid(SWIGLU_ALPHA * glu) * (lin + 1.0)


def moe(h, w_router, b_router, w_gate_up, b_gate_up, w_down, b_down):
    b, s, d = h.shape
    n_tok = b * s
    n_assign = n_tok * TOP_K
    xt = h.reshape(n_tok, d)
    logits = (xt @ w_router + b_router).astype(jnp.float32)
    top_logit, top_e = lax.top_k(logits, TOP_K)
    gate = jax.nn.softmax(top_logit, axis=-1)

    e_flat = top_e.reshape(n_assign)
    tok_flat = jnp.repeat(jnp.arange(n_tok, dtype=jnp.int32), TOP_K)
    order = jnp.argsort(e_flat)
    e_sorted = e_flat[order]
    tok_sorted = tok_flat[order]
    gate_sorted = gate.reshape(n_assign)[order]
    counts = jnp.bincount(e_flat, length=N_EXPERTS)
    start = jnp.cumsum(counts) - counts
    padded = (counts + MOE_BLOCK - 1) // MOE_BLOCK * MOE_BLOCK
    pend = jnp.cumsum(padded)
    pstart = pend - padded
    dest = pstart[e_sorted] + jnp.arange(n_assign, dtype=jnp.int32) - start[e_sorted]
    n_blocks = -(-n_assign // MOE_BLOCK) + N_EXPERTS
    n_rows = n_blocks * MOE_BLOCK
    row_tok = jnp.full((n_rows,), n_tok, jnp.int32).at[dest].set(tok_sorted)
    row_gate = jnp.zeros((n_rows,), jnp.float32).at[dest].set(gate_sorted)
    block_e = jnp.minimum(jnp.searchsorted(pend, jnp.arange(n_blocks) * MOE_BLOCK, side="right"),
                          N_EXPERTS - 1)
    x_pad = jnp.concatenate([xt, jnp.zeros((1, d), xt.dtype)], axis=0)

    def block_step(acc, inp):
        rows, rgate, e = inp
        xb = x_pad[rows]
        gu = xb @ w_gate_up[e] + b_gate_up[e]
        yb = clamped_swiglu(gu) @ w_down[e] + b_down[e]
        return acc.at[rows].add(yb.astype(jnp.float32) * rgate[:, None]), None

    acc0 = jnp.zeros((n_tok + 1, d), jnp.float32)
    acc, _ = lax.scan(block_step, acc0, (row_tok.reshape(n_blocks, MOE_BLOCK),
                                         row_gate.reshape(n_blocks, MOE_BLOCK), block_e))
    return acc[:n_tok].astype(h.dtype).reshape(b, s, d)


def setup_inputs(seed: int = 0) -> dict:
    key = jax.random.key(seed)
    ks = jax.random.split(key, 25)
    f32 = jnp.float32

    def nrm(k, shape, scale):
        return jax.random.normal(k, shape, f32) * scale

    L = DEPTH
    return {
        "x": nrm(ks[0], (BATCH, SEQ, D_MODEL), 1.0),
        "mem": nrm(ks[1], (BATCH, N_MEM, D_MODEL), 1.0),
        "w_in": nrm(ks[2], (L, D_MODEL, D_IN), D_MODEL ** -0.5),
        "b_in": nrm(ks[3], (L, D_IN), 0.01),
        "w_gla_a2": nrm(ks[4], (L, GLA_RANK, GLA_KEY), GLA_RANK ** -0.5),
        "b_gla_a": nrm(ks[5], (L, GLA_KEY), 0.01),
        "gla_norm_g": 1.0 + nrm(ks[6], (L, GLA_DV), 0.01),
        "hgrn_norm_g": 1.0 + nrm(ks[7], (L, HGRN_DV), 0.01),
        "hgrn_lb_logits": nrm(ks[8], (L + 1, HGRN_KEY), 0.1),
        "w_mix_o": nrm(ks[9], (L, GLA_VAL, D_MODEL), GLA_VAL ** -0.5 * DEEPNORM_BETA),
        "w_xq": nrm(ks[10], (L, D_MODEL, D_MODEL), D_MODEL ** -0.5),
        "w_mem_kv": nrm(ks[11], (L, D_MODEL, 2 * D_MODEL), D_MODEL ** -0.5),
        "w_xo": nrm(ks[12], (L, D_MODEL, D_MODEL), D_MODEL ** -0.5 * DEEPNORM_BETA),
        "w_router": nrm(ks[13], (L, D_MODEL, N_EXPERTS), D_MODEL ** -0.5),
        "b_router": nrm(ks[14], (L, N_EXPERTS), 0.01),
        "w_gate_up": nrm(ks[15], (L, N_EXPERTS, D_MODEL, 2 * D_EXPERT), D_MODEL ** -0.5),
        "b_gate_up": nrm(ks[16], (L, N_EXPERTS, 2 * D_EXPERT), 0.01),
        "w_down": nrm(ks[17], (L, N_EXPERTS, D_EXPERT, D_MODEL), D_EXPERT ** -0.5 * DEEPNORM_BETA),
        "b_down": nrm(ks[18], (L, N_EXPERTS, D_MODEL), 0.01),
        "ln1_g": 1.0 + nrm(ks[19], (L, D_MODEL), 0.01),
        "ln1_b": nrm(ks[20], (L, D_MODEL), 0.01),
        "ln2_g": 1.0 + nrm(ks[21], (L, D_MODEL), 0.01),
        "ln2_b": nrm(ks[22], (L, D_MODEL), 0.01),
        "ln3_g": 1.0 + nrm(ks[23], (L, D_MODEL), 0.01),
        "ln3_b": nrm(ks[24], (L, D_MODEL), 0.01),
    }


def reference(x, mem, w_in, b_in, w_gla_a2, b_gla_a, gla_norm_g, hgrn_norm_g, hgrn_lb_logits,
              w_mix_o, w_xq, w_mem_kv, w_xo, w_router, b_router, w_gate_up, b_gate_up,
              w_down, b_down, ln1_g, ln1_b, ln2_g, ln2_b, ln3_g, ln3_b):
    lb_table = jnp.cumsum(jax.nn.softmax(hgrn_lb_logits.astype(jnp.float32), axis=0), axis=0)
    h = x
    for l in range(DEPTH):
        mix = token_mixer(h, w_in[l], b_in[l], w_gla_a2[l], b_gla_a[l], gla_norm_g[l],
                          hgrn_norm_g[l], lb_table[l], w_mix_o[l])
        h = layer_norm(DEEPNORM_ALPHA * h + mix, ln1_g[l], ln1_b[l])
        xat = memory_cross_attention(h, mem, w_xq[l], w_mem_kv[l], w_xo[l])
        h = layer_norm(DEEPNORM_ALPHA * h + xat, ln2_g[l], ln2_b[l])
        ffn = moe(h, w_router[l], b_router[l], w_gate_up[l], b_gate_up[l], w_down[l], b_down[l])
        h = layer_norm(DEEPNORM_ALPHA * h + ffn, ln3_g[l], ln3_b[l])
    return h
```

```python
import functools
import math

import jax
import jax.numpy as jnp
from jax import lax
from jax.experimental import pallas as pl
from jax.experimental.pallas import tpu as pltpu

F32 = jnp.float32
BF16 = jnp.bfloat16

D_MODEL = 2048
N_MEM = 256
GLA_HEADS = 4
GLA_KEY = D_MODEL // 2
GLA_VAL = D_MODEL
GLA_DK = GLA_KEY // GLA_HEADS
GLA_DV = GLA_VAL // GLA_HEADS
GLA_RANK = 16
GLA_TAU = 16.0
HGRN_DK = 128
HGRN_HEADS = D_MODEL // HGRN_DK
HGRN_DV = D_MODEL // HGRN_HEADS
X_HEADS = 4
X_DH = D_MODEL // X_HEADS
N_EXPERTS = 32
TOP_K = 4
D_EXPERT = D_MODEL
SWIGLU_LIMIT = 7.0
SWIGLU_ALPHA = 1.702
DEPTH = 1
DEEPNORM_ALPHA = (2.0 * DEPTH) ** 0.25
NORM_EPS = 1e-5

LANES = 128
CHUNK = 64
SUB = 16
N_SUB = CHUNK // SUB

OFF_GQ = 0
OFF_GK = GLA_KEY
OFF_GV = 2 * GLA_KEY
OFF_GR = OFF_GV + GLA_VAL
OFF_HQ = OFF_GR + GLA_VAL
OFF_HF = OFF_HQ + D_MODEL
OFF_HI = OFF_HF + D_MODEL
OFF_HG = OFF_HI + D_MODEL
OFF_MA = OFF_HG + D_MODEL
OFF_MB = OFF_MA + D_MODEL
OFF_A1 = OFF_MB + D_MODEL
PROJ_TN = 512
D_IN_PAD = -(-(OFF_A1 + LANES) // PROJ_TN) * PROJ_TN
GA1_SRC = OFF_GR + GLA_VAL

VMEM_LIMIT = 56 * 1024 * 1024


def _cparams(sem):
    return pltpu.CompilerParams(dimension_semantics=sem, vmem_limit_bytes=VMEM_LIMIT)


def _dot(a, b):
    return jnp.dot(a, b, preferred_element_type=F32)


def _dot_nt(a, b):
    return lax.dot_general(a, b, (((1,), (1,)), ((), ())), preferred_element_type=F32)


def _dot_tn(a, b):
    return lax.dot_general(a, b, (((0,), (0,)), ((), ())), preferred_element_type=F32)


def _sigmoid(x):
    return 1.0 / (1.0 + jnp.exp(-x))


def _log_sigmoid(x):
    return jnp.minimum(x, 0.0) - jnp.log1p(jnp.exp(-jnp.abs(x)))


def _layer_norm(r, g, b):
    mu = jnp.mean(r, axis=-1, keepdims=True)
    c = r - mu
    var = jnp.mean(c * c, axis=-1, keepdims=True)
    return c * lax.rsqrt(var + NORM_EPS) * g + b


def _split_bf16(x):
    hi = x.astype(BF16)
    lo = (x - hi.astype(F32)).astype(BF16)
    return hi, lo


def _proj_kernel(x_ref, w_ref, b_ref, o_ref, xb_ref):
    @pl.when(pl.program_id(1) == 0)
    def _():
        xb_ref[...] = x_ref[...].astype(BF16)

    o_ref[...] = (_dot(xb_ref[...], w_ref[...]) + b_ref[...]).astype(o_ref.dtype)


def _matmul_bias(x, w, b, tm, tn, out_dtype):
    m, k = x.shape
    n = w.shape[1]
    tm = min(tm, m)
    return pl.pallas_call(
        _proj_kernel,
        grid=(m // tm, n // tn),
        in_specs=[pl.BlockSpec((tm, k), lambda i, j: (i, 0)),
                  pl.BlockSpec((k, tn), lambda i, j: (0, j)),
                  pl.BlockSpec((1, tn), lambda i, j: (0, j))],
        out_specs=pl.BlockSpec((tm, tn), lambda i, j: (i, j)),
        out_shape=jax.ShapeDtypeStruct((m, n), out_dtype),
        scratch_shapes=[pltpu.VMEM((tm, k), BF16)],
        compiler_params=_cparams(("arbitrary", "arbitrary")),
        name="matmul_bias",
    )(x, w, b)


def _cum_matrix():
    r = lax.broadcasted_iota(jnp.int32, (2 * CHUNK, CHUNK), 0)
    c = lax.broadcasted_iota(jnp.int32, (2 * CHUNK, CHUNK), 1)
    incl = c <= r
    rr = r - CHUNK
    start = c < (rr // SUB) * SUB
    one = jnp.logical_or(jnp.logical_and(r < CHUNK, incl), jnp.logical_and(r >= CHUNK, start))
    return jnp.where(one, 1.0, 0.0).astype(BF16)


def _chunk_step(q, k, v, lg, state, cmat):
    dk = q.shape[1]
    hi, lo = _split_bf16(lg)
    c2 = _dot(cmat, hi) + _dot(cmat, lo)
    cum = c2[:CHUNK]
    cst = c2[CHUNK:]
    qe = q * jnp.exp(cum - cst)
    o = _dot((qe * jnp.exp(cst)).astype(BF16), state.astype(BF16))
    qe_b = qe.astype(BF16)
    intra = []
    for i in range(N_SUB):
        n = SUB * (i + 1)
        c_i = cst[SUB * i:SUB * i + 1, :]
        r_i = (k[:n] * jnp.exp(c_i - cum[:n])).astype(BF16)
        a = _dot_nt(qe_b[SUB * i:SUB * (i + 1)], r_i)
        row = lax.broadcasted_iota(jnp.int32, (SUB, n), 0)
        col = lax.broadcasted_iota(jnp.int32, (SUB, n), 1)
        a = jnp.where(col - SUB * i <= row, a, 0.0)
        intra.append(_dot(a.astype(BF16), v[:n]))
    o = o + jnp.concatenate(intra, axis=0)
    last = cum[CHUNK - 1:CHUNK, :]
    kd = (k * jnp.exp(last - cum)).astype(BF16)
    e_col = jnp.transpose(jnp.broadcast_to(jnp.exp(last), (LANES, dk)))[:, :1]
    new_state = state * e_col + _dot_tn(kd, v)
    return o, new_state


def _head_rmsnorm(o, g):
    return o * lax.rsqrt(jnp.mean(o * o, axis=-1, keepdims=True) + NORM_EPS) * g


def _gla_kernel(q_ref, k_ref, v_ref, r_ref, ma_ref, a1_ref, wa2_ref, ba_ref, g_ref, o_ref, s_ref):
    @pl.when(pl.program_id(2) == 0)
    def _():
        s_ref[...] = jnp.zeros_like(s_ref)

    cmat = _cum_matrix()
    n_chunks = q_ref.shape[0] // CHUNK

    def body(c, carry):
        rows = pl.ds(pl.multiple_of(c * CHUNK, CHUNK), CHUNK)
        z = _dot(a1_ref[rows, :], wa2_ref[...]) + ba_ref[...]
        lg = _log_sigmoid(z) * (1.0 / GLA_TAU)
        q = q_ref[rows, :].astype(F32) * (GLA_DK ** -0.5)
        k = k_ref[rows, :].astype(F32)
        o, new_state = _chunk_step(q, k, v_ref[rows, :], lg, s_ref[...], cmat)
        s_ref[...] = new_state
        o = _head_rmsnorm(o, g_ref[...])
        r = r_ref[rows, :].astype(F32)
        gate = _sigmoid(ma_ref[rows, :].astype(F32))
        o_ref[rows, :] = (gate * (o * (r * _sigmoid(r)))).astype(o_ref.dtype)
        return carry

    lax.fori_loop(0, n_chunks, body, 0)


def _gla_branch(proj, wa2p, ba, gla_g, batch, seq, blk):
    nb = seq // blk
    row = lambda b, h, c: b * nb + c
    kb, vb = GLA_DK, GLA_DV
    in_specs = [
        pl.BlockSpec((blk, kb), lambda b, h, c: (row(b, h, c), OFF_GQ // kb + h)),
        pl.BlockSpec((blk, kb), lambda b, h, c: (row(b, h, c), OFF_GK // kb + h)),
        pl.BlockSpec((blk, vb), lambda b, h, c: (row(b, h, c), OFF_GV // vb + h)),
        pl.BlockSpec((blk, vb), lambda b, h, c: (row(b, h, c), OFF_GR // vb + h)),
        pl.BlockSpec((blk, vb), lambda b, h, c: (row(b, h, c), OFF_MA // vb + h)),
        pl.BlockSpec((blk, LANES), lambda b, h, c: (row(b, h, c), OFF_A1 // LANES)),
        pl.BlockSpec((LANES, kb), lambda b, h, c: (0, h)),
        pl.BlockSpec((1, kb), lambda b, h, c: (0, h)),
        pl.BlockSpec((1, vb), lambda b, h, c: (0, 0)),
    ]
    return pl.pallas_call(
        _gla_kernel,
        grid=(batch, GLA_HEADS, nb),
        in_specs=in_specs,
        out_specs=pl.BlockSpec((blk, vb), lambda b, h, c: (row(b, h, c), h)),
        out_shape=jax.ShapeDtypeStruct((batch * seq, GLA_VAL), BF16),
        scratch_shapes=[pltpu.VMEM((kb, vb), F32)],
        compiler_params=_cparams(("arbitrary", "arbitrary", "arbitrary")),
        name="gla_scan",
    )(proj, proj, proj, proj, proj, proj, wa2p, ba, gla_g)


HGRN_HP = 2


def _hgrn_kernel(q_ref, f_ref, i_ref, g_ref, mb_ref, gla_ref, lbl_ref, ng_ref, o_ref, s_ref):
    @pl.when(pl.program_id(2) == 0)
    def _():
        s_ref[...] = jnp.zeros_like(s_ref)

    cmat = _cum_matrix()
    n_chunks = q_ref.shape[0] // CHUNK
    lbl = lbl_ref[...].astype(F32)
    e = jnp.exp(lbl - jnp.max(lbl, axis=0, keepdims=True))
    lb = e[0:1, :] / jnp.sum(e, axis=0, keepdims=True)

    def body(c, carry):
        rows = pl.ds(pl.multiple_of(c * CHUNK, CHUNK), CHUNK)
        hq = q_ref[rows, :].astype(F32)
        hf = f_ref[rows, :].astype(F32)
        q_all = hq * _sigmoid(hq)
        f_all = lb + (1.0 - lb) * _sigmoid(hf)
        lg_all = jnp.log(f_all)
        k_all = (1.0 - lb) * _sigmoid(-hf)
        v_all = i_ref[rows, :]
        outs = []
        for h in range(HGRN_HP):
            sl = slice(h * HGRN_DK, (h + 1) * HGRN_DK)
            o, new_state = _chunk_step(q_all[:, sl], k_all[:, sl], v_all[:, sl], lg_all[:, sl],
                                       s_ref[h], cmat)
            s_ref[h] = new_state
            outs.append(_head_rmsnorm(o, ng_ref[...]))
        o = jnp.concatenate(outs, axis=1)
        o = o * _sigmoid(g_ref[rows, :].astype(F32))
        merged = gla_ref[rows, :].astype(F32) + _sigmoid(mb_ref[rows, :].astype(F32)) * o
        o_ref[rows, :] = merged.astype(o_ref.dtype)
        return carry

    lax.fori_loop(0, n_chunks, body, 0)


def _hgrn_branch(proj, gla_out, lb_logits, hgrn_g, batch, seq, blk):
    nb = seq // blk
    w = HGRN_HP * HGRN_DK
    row = lambda b, h, c: b * nb + c
    in_specs = [
        pl.BlockSpec((blk, w), lambda b, h, c: (row(b, h, c), OFF_HQ // w + h)),
        pl.BlockSpec((blk, w), lambda b, h, c: (row(b, h, c), OFF_HF // w + h)),
        pl.BlockSpec((blk, w), lambda b, h, c: (row(b, h, c), OFF_HI // w + h)),
        pl.BlockSpec((blk, w), lambda b, h, c: (row(b, h, c), OFF_HG // w + h)),
        pl.BlockSpec((blk, w), lambda b, h, c: (row(b, h, c), OFF_MB // w + h)),
        pl.BlockSpec((blk, w), lambda b, h, c: (row(b, h, c), h)),
        pl.BlockSpec((DEPTH + 1, w), lambda b, h, c: (0, h)),
        pl.BlockSpec((1, HGRN_DV), lambda b, h, c: (0, 0)),
    ]
    return pl.pallas_call(
        _hgrn_kernel,
        grid=(batch, HGRN_HEADS // HGRN_HP, nb),
        in_specs=in_specs,
        out_specs=pl.BlockSpec((blk, w), lambda b, h, c: (row(b, h, c), h)),
        out_shape=jax.ShapeDtypeStruct((batch * seq, D_MODEL), BF16),
        scratch_shapes=[pltpu.VMEM((HGRN_HP, HGRN_DK, HGRN_DV), F32)],
        compiler_params=_cparams(("arbitrary", "arbitrary", "arbitrary")),
        name="hgrn_scan",
    )(proj, proj, proj, proj, proj, gla_out, lb_logits, hgrn_g)


def _out_ln_kernel(m_ref, w_ref, x_ref, g_ref, b_ref, o_ref):
    y = _dot(m_ref[...], w_ref[...])
    o_ref[...] = _layer_norm(DEEPNORM_ALPHA * x_ref[...] + y, g_ref[...], b_ref[...])


def _resident(shape):
    return pl.BlockSpec(shape, lambda *_: (0,) * len(shape), pipeline_mode=pl.Buffered(1))


def _out_ln(merged, w_o, x2, g, b, tm):
    t = merged.shape[0]
    tm = min(tm, t)
    return pl.pallas_call(
        _out_ln_kernel,
        grid=(t // tm,),
        in_specs=[pl.BlockSpec((tm, D_MODEL), lambda i: (i, 0)),
                  _resident((D_MODEL, D_MODEL)),
                  pl.BlockSpec((tm, D_MODEL), lambda i: (i, 0)),
                  _resident((1, D_MODEL)), _resident((1, D_MODEL))],
        out_specs=pl.BlockSpec((tm, D_MODEL), lambda i: (i, 0)),
        out_shape=jax.ShapeDtypeStruct((t, D_MODEL), F32),
        compiler_params=_cparams(("arbitrary",)),
        name="wo_ln1",
    )(merged, w_o, x2, g, b)


def _xattn_kernel(h_ref, wq_ref, kv_ref, wo_ref, g_ref, b_ref, o_ref):
    h = h_ref[...]
    q = _dot(h.astype(BF16), wq_ref[...]).astype(BF16)
    outs = []
    for hh in range(X_HEADS):
        sl = slice(hh * X_DH, (hh + 1) * X_DH)
        kh = kv_ref[0, :, sl]
        vh = kv_ref[0, :, D_MODEL + hh * X_DH:D_MODEL + (hh + 1) * X_DH]
        s = _dot_nt(q[:, sl], kh) * (X_DH ** -0.5)
        p = jnp.exp(s - jnp.max(s, axis=-1, keepdims=True))
        l = jnp.sum(p, axis=-1, keepdims=True)
        outs.append((_dot(p.astype(BF16), vh) / l).astype(BF16))
    o = jnp.concatenate(outs, axis=1)
    y = _dot(o, wo_ref[...])
    o_ref[...] = _layer_norm(DEEPNORM_ALPHA * h + y, g_ref[...], b_ref[...])


def _xattn(h1, w_xq, kv, w_xo, g, b, seq, tm):
    t = h1.shape[0]
    tm = min(tm, seq)
    per_b = seq // tm
    return pl.pallas_call(
        _xattn_kernel,
        grid=(t // tm,),
        in_specs=[pl.BlockSpec((tm, D_MODEL), lambda i: (i, 0)),
                  _resident((D_MODEL, D_MODEL)),
                  pl.BlockSpec((1, N_MEM, 2 * D_MODEL), lambda i: (i // per_b, 0, 0)),
                  _resident((D_MODEL, D_MODEL)),
                  _resident((1, D_MODEL)), _resident((1, D_MODEL))],
        out_specs=pl.BlockSpec((tm, D_MODEL), lambda i: (i, 0)),
        out_shape=jax.ShapeDtypeStruct((t, D_MODEL), F32),
        compiler_params=_cparams(("arbitrary",)),
        name="xattn_ln2",
    )(h1, w_xq, kv, w_xo, g, b)


def _router_kernel(h_ref, w_ref, b_ref, idx_ref, gate_ref):
    hh, hl = _split_bf16(h_ref[...])
    wh, wl = _split_bf16(w_ref[...])
    logits = _dot(hh, wh) + _dot(hh, wl) + _dot(hl, wh) + b_ref[...]
    lane = lax.broadcasted_iota(jnp.int32, logits.shape, 1)
    neg = jnp.float32(-jnp.inf)
    logits = jnp.where(lane < N_EXPERTS, logits, neg)
    idx_out = jnp.zeros(logits.shape, jnp.int32)
    tops = []
    for k in range(TOP_K):
        m = jnp.max(logits, axis=-1, keepdims=True)
        idx = jnp.min(jnp.where(logits == m, lane, LANES), axis=-1, keepdims=True)
        idx_out = jnp.where(lane == k, idx, idx_out)
        logits = jnp.where(lane == idx, neg, logits)
        tops.append(m)
    ex = [jnp.exp(m - tops[0]) for m in tops]
    den = ex[0] + ex[1] + ex[2] + ex[3]
    gate_out = jnp.zeros(logits.shape, F32)
    for k in range(TOP_K):
        gate_out = jnp.where(lane == k, ex[k] / den, gate_out)
    idx_ref[...] = idx_out
    gate_ref[...] = gate_out


def _router(h2, w_rp, b_rp, tm):
    t = h2.shape[0]
    tm = min(tm, t)
    return pl.pallas_call(
        _router_kernel,
        grid=(t // tm,),
        in_specs=[pl.BlockSpec((tm, D_MODEL), lambda i: (i, 0)),
                  _resident((D_MODEL, LANES)), _resident((1, LANES))],
        out_specs=[pl.BlockSpec((tm, LANES), lambda i: (i, 0)),
                   pl.BlockSpec((tm, LANES), lambda i: (i, 0))],
        out_shape=[jax.ShapeDtypeStruct((t, LANES), jnp.int32),
                   jax.ShapeDtypeStruct((t, LANES), F32)],
        compiler_params=_cparams(("arbitrary",)),
        name="router_top4",
    )(h2, w_rp, b_rp)


MOE_TF = 256


def _moe_kernel(be_ref, nu_ref, tok_ref, h_hbm, wgu_ref, bgu_ref, wd_ref, bd_ref, o_ref,
                xf_ref, xb_ref, sem):
    blk = pl.program_id(0)
    f = pl.program_id(1)
    n_rows = xf_ref.shape[0]
    active = blk < nu_ref[0]

    def row_copy(r):
        return pltpu.make_async_copy(h_hbm.at[pl.ds(tok_ref[0, 0, r], 1), :],
                                     xf_ref.at[pl.ds(r, 1), :], sem)

    @pl.when(jnp.logical_and(active, f == 0))
    def _():
        def issue(r, c):
            row_copy(r).start()
            return c
        lax.fori_loop(0, n_rows, issue, 0)

        def drain(r, c):
            row_copy(r).wait()
            return c
        lax.fori_loop(0, n_rows, drain, 0)
        xb_ref[...] = xf_ref[...].astype(BF16)

    @pl.when(active)
    def _():
        gu = _dot(xb_ref[...], wgu_ref[0].astype(BF16)) + bgu_ref[0]
        glu = jnp.minimum(gu, SWIGLU_LIMIT)
        a = glu * _sigmoid(SWIGLU_ALPHA * glu)
        lin = jnp.clip(gu, -SWIGLU_LIMIT, SWIGLU_LIMIT) + 1.0
        w2 = gu.shape[1]
        prod = a * pltpu.roll(lin, w2 - 1, 1)
        r = lax.broadcasted_iota(jnp.int32, (w2, w2 // 2), 0)
        c = lax.broadcasted_iota(jnp.int32, (w2, w2 // 2), 1)
        sel = jnp.where(r == 2 * c, 1.0, 0.0).astype(BF16)
        act = _dot(prod.astype(BF16), sel).astype(BF16)
        y = _dot(act, wd_ref[0].astype(BF16))

        @pl.when(f == 0)
        def _():
            o_ref[...] = y + bd_ref[0]

        @pl.when(f != 0)
        def _():
            o_ref[...] += y


def _moe_ffn(h2, row_tok3, block_e, n_used, w_gate_up, b_gate_up3, w_down, b_down3, rows):
    n_blocks = row_tok3.shape[0]
    nf = D_EXPERT // MOE_TF
    ab = lambda b, nu: jnp.minimum(b, nu[0] - 1)
    grid_spec = pltpu.PrefetchScalarGridSpec(
        num_scalar_prefetch=2,
        grid=(n_blocks, nf),
        in_specs=[
            pl.BlockSpec((1, 1, rows), lambda b, f, be, nu: (ab(b, nu), 0, 0),
                         memory_space=pltpu.SMEM),
            pl.BlockSpec(memory_space=pl.ANY),
            pl.BlockSpec((1, D_MODEL, 2 * MOE_TF), lambda b, f, be, nu: (be[b], 0, f)),
            pl.BlockSpec((1, 1, 2 * MOE_TF), lambda b, f, be, nu: (be[b], 0, f)),
            pl.BlockSpec((1, MOE_TF, D_MODEL), lambda b, f, be, nu: (be[b], f, 0)),
            pl.BlockSpec((1, 1, D_MODEL), lambda b, f, be, nu: (be[b], 0, 0)),
        ],
        out_specs=pl.BlockSpec((rows, D_MODEL), lambda b, f, be, nu: (ab(b, nu), 0)),
        scratch_shapes=[pltpu.VMEM((rows, D_MODEL), F32),
                        pltpu.VMEM((rows, D_MODEL), BF16),
                        pltpu.SemaphoreType.DMA(())],
    )
    return pl.pallas_call(
        _moe_kernel,
        grid_spec=grid_spec,
        out_shape=jax.ShapeDtypeStruct((n_blocks * rows, D_MODEL), F32),
        compiler_params=_cparams(("arbitrary", "arbitrary")),
        name="moe_ffn",
    )(block_e, n_used, row_tok3, h2, w_gate_up, b_gate_up3, w_down, b_down3)


def _combine_kernel(dest_ref, y_hbm, h_ref, gate_ref, g_ref, b_ref, o_ref, yb_ref, sem):
    tm = h_ref.shape[0]
    n = TOP_K * tm

    def row_copy(j):
        return pltpu.make_async_copy(y_hbm.at[pl.ds(dest_ref[0, 0, j], 1), :],
                                     yb_ref.at[pl.ds(j, 1), :], sem)

    def issue(j, c):
        row_copy(j).start()
        return c
    lax.fori_loop(0, n, issue, 0)

    def drain(j, c):
        row_copy(j).wait()
        return c
    lax.fori_loop(0, n, drain, 0)

    gate = gate_ref[...]
    acc = DEEPNORM_ALPHA * h_ref[...]
    for k in range(TOP_K):
        acc = acc + gate[:, k:k + 1] * yb_ref[k * tm:(k + 1) * tm, :]
    o_ref[...] = _layer_norm(acc, g_ref[...], b_ref[...])


def _combine(dest3, y_sorted, h2, gates, g, b, tm):
    t = h2.shape[0]
    return pl.pallas_call(
        _combine_kernel,
        grid=(t // tm,),
        in_specs=[pl.BlockSpec((1, 1, TOP_K * tm), lambda i: (i, 0, 0), memory_space=pltpu.SMEM),
                  pl.BlockSpec(memory_space=pl.ANY),
                  pl.BlockSpec((tm, D_MODEL), lambda i: (i, 0)),
                  pl.BlockSpec((tm, LANES), lambda i: (i, 0)),
                  _resident((1, D_MODEL)), _resident((1, D_MODEL))],
        out_specs=pl.BlockSpec((tm, D_MODEL), lambda i: (i, 0)),
        out_shape=jax.ShapeDtypeStruct((t, D_MODEL), F32),
        scratch_shapes=[pltpu.VMEM((TOP_K * tm, D_MODEL), F32), pltpu.SemaphoreType.DMA(())],
        compiler_params=_cparams(("arbitrary",)),
        name="moe_combine_ln3",
    )(dest3, y_sorted, h2, gates, g, b)


def _dispatch_plan(top_e, n_tok, rows):
    n_assign = n_tok * TOP_K
    e_flat = top_e.reshape(n_assign)
    onehot = (e_flat[:, None] == jnp.arange(N_EXPERTS, dtype=jnp.int32)[None, :]).astype(jnp.int32)
    csum = jnp.cumsum(onehot, axis=0)
    pos = jnp.sum(csum * onehot, axis=1) - 1
    counts = csum[-1]
    padded = (counts + rows - 1) // rows * rows
    pend = jnp.cumsum(padded)
    pstart = pend - padded
    dest = (pstart[e_flat] + pos).astype(jnp.int32)
    n_blocks = n_assign // rows + N_EXPERTS
    tok_flat = jnp.arange(n_assign, dtype=jnp.int32) // TOP_K
    row_tok = jnp.zeros((n_blocks * rows,), jnp.int32).at[dest].set(tok_flat)
    n_used = (pend[-1] // rows).astype(jnp.int32)
    blk_start = jnp.arange(n_blocks, dtype=jnp.int32) * rows
    block_e = jnp.minimum(jnp.sum((pend[None, :] <= blk_start[:, None]).astype(jnp.int32), axis=1),
                          N_EXPERTS - 1)
    block_e = jnp.where(jnp.arange(n_blocks) < n_used, block_e, block_e[n_used - 1]).astype(jnp.int32)
    return dest, row_tok.reshape(n_blocks, 1, rows), block_e, n_used.reshape(1)


def kernel(x, mem, w_in, b_in, w_gla_a2, b_gla_a, gla_norm_g, hgrn_norm_g, hgrn_lb_logits, w_mix_o,
           w_xq, w_mem_kv, w_xo, w_router, b_router, w_gate_up, b_gate_up, w_down, b_down,
           ln1_g, ln1_b, ln2_g, ln2_b, ln3_g, ln3_b):
    batch, seq, d = x.shape
    t = batch * seq
    x2 = x.reshape(t, d)
    l = 0

    pad = D_IN_PAD - OFF_A1 - GLA_RANK
    w = w_in[l]
    w_in_p = jnp.concatenate([w[:, :GA1_SRC], w[:, GA1_SRC + GLA_RANK:], w[:, GA1_SRC:GA1_SRC + GLA_RANK],
                              jnp.zeros((d, pad), w.dtype)], axis=1).astype(BF16)
    bi = b_in[l]
    b_in_p = jnp.concatenate([bi[:GA1_SRC], bi[GA1_SRC + GLA_RANK:], bi[GA1_SRC:GA1_SRC + GLA_RANK],
                              jnp.zeros((pad,), bi.dtype)])[None, :]
    proj = _matmul_bias(x2, w_in_p, b_in_p, 1024, PROJ_TN, BF16)

    wa2p = jnp.concatenate([w_gla_a2[l], jnp.zeros((LANES - GLA_RANK, GLA_KEY), F32)], axis=0).astype(BF16)
    blk = min(512, seq)
    gla_out = _gla_branch(proj, wa2p, b_gla_a[l][None, :], gla_norm_g[l][None, :], batch, seq, blk)
    merged = _hgrn_branch(proj, gla_out, hgrn_lb_logits, hgrn_norm_g[l][None, :], batch, seq, blk)

    h1 = _out_ln(merged, w_mix_o[l].astype(BF16), x2, ln1_g[l][None, :], ln1_b[l][None, :], 512)

    kv = _matmul_bias(mem.reshape(batch * N_MEM, d), w_mem_kv[l].astype(BF16),
                      jnp.zeros((1, 2 * d), F32), 512, 512, BF16).reshape(batch, N_MEM, 2 * d)
    h2 = _xattn(h1, w_xq[l].astype(BF16), kv, w_xo[l].astype(BF16),
                ln2_g[l][None, :], ln2_b[l][None, :], seq, 512)

    w_rp = jnp.concatenate([w_router[l], jnp.zeros((d, LANES - N_EXPERTS), F32)], axis=1)
    b_rp = jnp.concatenate([b_router[l], jnp.zeros((LANES - N_EXPERTS,), F32)])[None, :]
    idx, gates = _router(h2, w_rp, b_rp, 512)

    rows = min(1024, t)
    dest, row_tok3, block_e, n_used = _dispatch_plan(idx[:, :TOP_K], t, rows)
    y_sorted = _moe_ffn(h2, row_tok3, block_e, n_used, w_gate_up[l],
                        b_gate_up[l][:, None, :], w_down[l], b_down[l][:, None, :], rows)

    tm_c = min(256, t)
    dest3 = dest.reshape(t // tm_c, tm_c, TOP_K).transpose(0, 2, 1).reshape(t // tm_c, 1, TOP_K * tm_c)
    out = _combine(dest3, y_sorted, h2, gates, ln3_g[l][None, :], ln3_b[l][None, :], tm_c)
    return out.reshape(batch, seq, d)
```

```python
import jax
import jax.numpy as jnp
from jax import lax
from jax.experimental import pallas as pl
from jax.experimental.pallas import tpu as pltpu

F32 = jnp.float32
BF16 = jnp.bfloat16

D_MODEL = 2048
N_MEM = 256
GLA_HEADS = 4
GLA_KEY = D_MODEL // 2
GLA_VAL = D_MODEL
GLA_DK = GLA_KEY // GLA_HEADS
GLA_DV = GLA_VAL // GLA_HEADS
GLA_RANK = 16
GLA_TAU = 16.0
HGRN_DK = 128
HGRN_HEADS = D_MODEL // HGRN_DK
HGRN_DV = D_MODEL // HGRN_HEADS
X_HEADS = 4
X_DH = D_MODEL // X_HEADS
N_EXPERTS = 32
TOP_K = 4
D_EXPERT = D_MODEL
SWIGLU_LIMIT = 7.0
SWIGLU_ALPHA = 1.702
DEPTH = 1
DEEPNORM_ALPHA = (2.0 * DEPTH) ** 0.25
NORM_EPS = 1e-5

LANES = 128
CHUNK = 64
SUB = 16
N_SUB = CHUNK // SUB

OFF_GQ = 0
OFF_GK = GLA_KEY
OFF_GV = 2 * GLA_KEY
OFF_GR = OFF_GV + GLA_VAL
OFF_HQ = OFF_GR + GLA_VAL
OFF_HF = OFF_HQ + D_MODEL
OFF_HI = OFF_HF + D_MODEL
OFF_HG = OFF_HI + D_MODEL
OFF_MA = OFF_HG + D_MODEL
OFF_MB = OFF_MA + D_MODEL
OFF_A1 = OFF_MB + D_MODEL
PROJ_TN = 512
D_IN_PAD = -(-(OFF_A1 + LANES) // PROJ_TN) * PROJ_TN
GA1_SRC = OFF_GR + GLA_VAL

VMEM_LIMIT = 56 * 1024 * 1024


def _cparams(sem):
    return pltpu.CompilerParams(dimension_semantics=sem, vmem_limit_bytes=VMEM_LIMIT)


def _dot(a, b):
    return jnp.dot(a, b, preferred_element_type=F32)


def _dot_nt(a, b):
    return lax.dot_general(a, b, (((1,), (1,)), ((), ())), preferred_element_type=F32)


def _dot_tn(a, b):
    return lax.dot_general(a, b, (((0,), (0,)), ((), ())), preferred_element_type=F32)


def _sigmoid(x):
    return 1.0 / (1.0 + jnp.exp(-x))


def _log_sigmoid(x):
    return jnp.minimum(x, 0.0) - jnp.log1p(jnp.exp(-jnp.abs(x)))


def _layer_norm(r, g, b):
    mu = jnp.mean(r, axis=-1, keepdims=True)
    c = r - mu
    var = jnp.mean(c * c, axis=-1, keepdims=True)
    return c * lax.rsqrt(var + NORM_EPS) * g + b


def _split_bf16(x):
    hi = x.astype(BF16)
    lo = (x - hi.astype(F32)).astype(BF16)
    return hi, lo


def _proj_kernel(x_ref, w_ref, b_ref, o_ref, xb_ref):
    @pl.when(pl.program_id(1) == 0)
    def _():
        xb_ref[...] = x_ref[...].astype(BF16)

    o_ref[...] = (_dot(xb_ref[...], w_ref[...]) + b_ref[...]).astype(o_ref.dtype)


def _matmul_bias(x, w, b, tm, tn, out_dtype):
    m, k = x.shape
    n = w.shape[1]
    tm = min(tm, m)
    return pl.pallas_call(
        _proj_kernel,
        grid=(m // tm, n // tn),
        in_specs=[pl.BlockSpec((tm, k), lambda i, j: (i, 0)),
                  pl.BlockSpec((k, tn), lambda i, j: (0, j)),
                  pl.BlockSpec((1, tn), lambda i, j: (0, j))],
        out_specs=pl.BlockSpec((tm, tn), lambda i, j: (i, j)),
        out_shape=jax.ShapeDtypeStruct((m, n), out_dtype),
        scratch_shapes=[pltpu.VMEM((tm, k), BF16)],
        compiler_params=_cparams(("arbitrary", "arbitrary")),
        name="matmul_bias",
    )(x, w, b)


def _cum_matrix():
    r = lax.broadcasted_iota(jnp.int32, (2 * CHUNK, CHUNK), 0)
    c = lax.broadcasted_iota(jnp.int32, (2 * CHUNK, CHUNK), 1)
    incl = c <= r
    rr = r - CHUNK
    start = c < (rr // SUB) * SUB
    one = jnp.logical_or(jnp.logical_and(r < CHUNK, incl), jnp.logical_and(r >= CHUNK, start))
    return jnp.where(one, 1.0, 0.0).astype(BF16)


def _chunk_rows(x, r):
    n_chunks = x.shape[0] // CHUNK
    return jnp.concatenate(
        [jnp.broadcast_to(x[c * CHUNK + r:c * CHUNK + r + 1, :], (CHUNK, x.shape[1]))
         for c in range(n_chunks)], axis=0)


def _scan_block(q, k, v, lg, states, n_heads):
    rows, dkt = q.shape
    dk = dkt // n_heads
    dv = v.shape[1] // n_heads
    n_chunks = rows // CHUNK
    cmat = _cum_matrix()
    hi, lo = _split_bf16(lg)
    cums, csts = [], []
    for c in range(n_chunks):
        rs = slice(c * CHUNK, (c + 1) * CHUNK)
        c2 = _dot(cmat, hi[rs]) + _dot(cmat, lo[rs])
        cums.append(c2[:CHUNK])
        csts.append(c2[CHUNK:])
    cum = jnp.concatenate(cums, axis=0)
    cst = jnp.concatenate(csts, axis=0)
    last = _chunk_rows(cum, CHUNK - 1)
    qe = q * jnp.exp(cum - cst)
    qd = (qe * jnp.exp(cst)).astype(BF16)
    kd = (k * jnp.exp(last - cum)).astype(BF16)
    row_in = lax.broadcasted_iota(jnp.int32, (rows, dkt), 0) & (CHUNK - 1)
    q_sub, r_sub = [], []
    for i in range(N_SUB):
        c_i = _chunk_rows(cst, SUB * i)
        r_sub.append(jnp.where(row_in < SUB * (i + 1), k * jnp.exp(c_i - cum), 0.0).astype(BF16))
        q_sub.append(jnp.where((row_in >> 4) == i, qe, 0.0).astype(BF16))
    causal = (lax.broadcasted_iota(jnp.int32, (CHUNK, CHUNK), 1)
              <= lax.broadcasted_iota(jnp.int32, (CHUNK, CHUNK), 0))
    intra = [[None] * n_heads for _ in range(n_chunks)]
    upd = [[None] * n_heads for _ in range(n_chunks)]
    for h in range(n_heads):
        ks = slice(h * dk, (h + 1) * dk)
        vs = slice(h * dv, (h + 1) * dv)
        q_stack = jnp.concatenate([x[:, ks] for x in q_sub], axis=1)
        r_stack = jnp.concatenate([x[:, ks] for x in r_sub], axis=1)
        for c in range(n_chunks):
            rs = slice(c * CHUNK, (c + 1) * CHUNK)
            a = _dot_nt(q_stack[rs], r_stack[rs])
            a = jnp.where(causal, a, 0.0).astype(BF16)
            intra[c][h] = _dot(a, v[rs, vs])
            upd[c][h] = _dot_tn(v[rs, vs], kd[rs, ks])
    outs = [[None] * n_heads for _ in range(n_chunks)]
    new_states = []
    for h in range(n_heads):
        ks = slice(h * dk, (h + 1) * dk)
        s = states[h]
        for c in range(n_chunks):
            rs = slice(c * CHUNK, (c + 1) * CHUNK)
            outs[c][h] = _dot_nt(qd[rs, ks], s.astype(BF16)) + intra[c][h]
            e_last = jnp.exp(cum[(c + 1) * CHUNK - 1:(c + 1) * CHUNK, ks])
            s = s * e_last + upd[c][h]
        new_states.append(s)
    o = jnp.concatenate([jnp.concatenate(row, axis=1) for row in outs], axis=0)
    return o, new_states


def _group_rmsnorm(o, g, n_heads):
    dv = o.shape[1] // n_heads
    parts = []
    for h in range(n_heads):
        oh = o[:, h * dv:(h + 1) * dv]
        parts.append(oh * lax.rsqrt(jnp.mean(oh * oh, axis=-1, keepdims=True) + NORM_EPS) * g)
    return jnp.concatenate(parts, axis=1)


def _gla_kernel(q_ref, k_ref, v_ref, r_ref, ma_ref, a1_ref, wa2_ref, ba_ref, g_ref, o_ref, s_ref):
    @pl.when(pl.program_id(2) == 0)
    def _():
        s_ref[...] = jnp.zeros_like(s_ref)

    z = _dot(a1_ref[...], wa2_ref[...]) + ba_ref[...]
    lg = _log_sigmoid(z) * (1.0 / GLA_TAU)
    q = q_ref[...].astype(F32) * (GLA_DK ** -0.5)
    k = k_ref[...].astype(F32)
    o, new_states = _scan_block(q, k, v_ref[...], lg, [s_ref[...]], 1)
    s_ref[...] = new_states[0]
    o = _group_rmsnorm(o, g_ref[...], 1)
    r = r_ref[...].astype(F32)
    gate = _sigmoid(ma_ref[...].astype(F32))
    o_ref[...] = (gate * (o * (r * _sigmoid(r)))).astype(o_ref.dtype)


def _gla_branch(proj, wa2p, ba, gla_g, batch, seq, blk):
    nb = seq // blk
    row = lambda b, h, c: b * nb + c
    kb, vb = GLA_DK, GLA_DV
    in_specs = [
        pl.BlockSpec((blk, kb), lambda b, h, c: (row(b, h, c), OFF_GQ // kb + h)),
        pl.BlockSpec((blk, kb), lambda b, h, c: (row(b, h, c), OFF_GK // kb + h)),
        pl.BlockSpec((blk, vb), lambda b, h, c: (row(b, h, c), OFF_GV // vb + h)),
        pl.BlockSpec((blk, vb), lambda b, h, c: (row(b, h, c), OFF_GR // vb + h)),
        pl.BlockSpec((blk, vb), lambda b, h, c: (row(b, h, c), OFF_MA // vb + h)),
        pl.BlockSpec((blk, LANES), lambda b, h, c: (row(b, h, c), OFF_A1 // LANES)),
        pl.BlockSpec((LANES, kb), lambda b, h, c: (0, h)),
        pl.BlockSpec((1, kb), lambda b, h, c: (0, h)),
        pl.BlockSpec((1, vb), lambda b, h, c: (0, 0)),
    ]
    return pl.pallas_call(
        _gla_kernel,
        grid=(batch, GLA_HEADS, nb),
        in_specs=in_specs,
        out_specs=pl.BlockSpec((blk, vb), lambda b, h, c: (row(b, h, c), h)),
        out_shape=jax.ShapeDtypeStruct((batch * seq, GLA_VAL), BF16),
        scratch_shapes=[pltpu.VMEM((vb, kb), F32)],
        compiler_params=_cparams(("arbitrary", "arbitrary", "arbitrary")),
        name="gla_scan",
    )(proj, proj, proj, proj, proj, proj, wa2p, ba, gla_g)


HGRN_HP = 2


def _hgrn_kernel(q_ref, f_ref, i_ref, g_ref, mb_ref, gla_ref, lbl_ref, ng_ref, o_ref, s_ref):
    @pl.when(pl.program_id(2) == 0)
    def _():
        s_ref[...] = jnp.zeros_like(s_ref)

    lbl = lbl_ref[...].astype(F32)
    e = jnp.exp(lbl - jnp.max(lbl, axis=0, keepdims=True))
    lb = e[0:1, :] / jnp.sum(e, axis=0, keepdims=True)

    hq = q_ref[...].astype(F32)
    hf = f_ref[...].astype(F32)
    q = hq * _sigmoid(hq)
    lg = jnp.log(lb + (1.0 - lb) * _sigmoid(hf))
    k = (1.0 - lb) * _sigmoid(-hf)
    o, new_states = _scan_block(q, k, i_ref[...], lg, [s_ref[h] for h in range(HGRN_HP)], HGRN_HP)
    for h in range(HGRN_HP):
        s_ref[h] = new_states[h]
    o = _group_rmsnorm(o, ng_ref[...], HGRN_HP)
    o = o * _sigmoid(g_ref[...].astype(F32))
    merged = gla_ref[...].astype(F32) + _sigmoid(mb_ref[...].astype(F32)) * o
    o_ref[...] = merged.astype(o_ref.dtype)


def _hgrn_branch(proj, gla_out, lb_logits, hgrn_g, batch, seq, blk):
    nb = seq // blk
    w = HGRN_HP * HGRN_DK
    row = lambda b, h, c: b * nb + c
    in_specs = [
        pl.BlockSpec((blk, w), lambda b, h, c: (row(b, h, c), OFF_HQ // w + h)),
        pl.BlockSpec((blk, w), lambda b, h, c: (row(b, h, c), OFF_HF // w + h)),
        pl.BlockSpec((blk, w), lambda b, h, c: (row(b, h, c), OFF_HI // w + h)),
        pl.BlockSpec((blk, w), lambda b, h, c: (row(b, h, c), OFF_HG // w + h)),
        pl.BlockSpec((blk, w), lambda b, h, c: (row(b, h, c), OFF_MB // w + h)),
        pl.BlockSpec((blk, w), lambda b, h, c: (row(b, h, c), h)),
        pl.BlockSpec((DEPTH + 1, w), lambda b, h, c: (0, h)),
        pl.BlockSpec((1, HGRN_DV), lambda b, h, c: (0, 0)),
    ]
    return pl.pallas_call(
        _hgrn_kernel,
        grid=(batch, HGRN_HEADS // HGRN_HP, nb),
        in_specs=in_specs,
        out_specs=pl.BlockSpec((blk, w), lambda b, h, c: (row(b, h, c), h)),
        out_shape=jax.ShapeDtypeStruct((batch * seq, D_MODEL), BF16),
        scratch_shapes=[pltpu.VMEM((HGRN_HP, HGRN_DV, HGRN_DK), F32)],
        compiler_params=_cparams(("arbitrary", "arbitrary", "arbitrary")),
        name="hgrn_scan",
    )(proj, proj, proj, proj, proj, gla_out, lb_logits, hgrn_g)


def _out_ln_kernel(m_ref, w_ref, x_ref, g_ref, b_ref, o_ref):
    y = _dot(m_ref[...], w_ref[...])
    o_ref[...] = _layer_norm(DEEPNORM_ALPHA * x_ref[...] + y, g_ref[...], b_ref[...])


def _resident(shape):
    return pl.BlockSpec(shape, lambda *_: (0,) * len(shape), pipeline_mode=pl.Buffered(1))


def _out_ln(merged, w_o, x2, g, b, tm):
    t = merged.shape[0]
    tm = min(tm, t)
    return pl.pallas_call(
        _out_ln_kernel,
        grid=(t // tm,),
        in_specs=[pl.BlockSpec((tm, D_MODEL), lambda i: (i, 0)),
                  _resident((D_MODEL, D_MODEL)),
                  pl.BlockSpec((tm, D_MODEL), lambda i: (i, 0)),
                  _resident((1, D_MODEL)), _resident((1, D_MODEL))],
        out_specs=pl.BlockSpec((tm, D_MODEL), lambda i: (i, 0)),
        out_shape=jax.ShapeDtypeStruct((t, D_MODEL), F32),
        compiler_params=_cparams(("arbitrary",)),
        name="wo_ln1",
    )(merged, w_o, x2, g, b)


def _xattn_kernel(h_ref, wq_ref, kv_ref, wo_ref, g_ref, b_ref, o_ref):
    h = h_ref[...]
    q = _dot(h.astype(BF16), wq_ref[...]).astype(BF16)
    outs = []
    for hh in range(X_HEADS):
        sl = slice(hh * X_DH, (hh + 1) * X_DH)
        kh = kv_ref[0, :, sl]
        vh = kv_ref[0, :, D_MODEL + hh * X_DH:D_MODEL + (hh + 1) * X_DH]
        s = _dot_nt(q[:, sl], kh) * (X_DH ** -0.5)
        p = jnp.exp(s - jnp.max(s, axis=-1, keepdims=True))
        l = jnp.sum(p, axis=-1, keepdims=True)
        outs.append((_dot(p.astype(BF16), vh) / l).astype(BF16))
    o = jnp.concatenate(outs, axis=1)
    y = _dot(o, wo_ref[...])
    o_ref[...] = _layer_norm(DEEPNORM_ALPHA * h + y, g_ref[...], b_ref[...])


def _xattn(h1, w_xq, kv, w_xo, g, b, seq, tm):
    t = h1.shape[0]
    tm = min(tm, seq)
    per_b = seq // tm
    return pl.pallas_call(
        _xattn_kernel,
        grid=(t // tm,),
        in_specs=[pl.BlockSpec((tm, D_MODEL), lambda i: (i, 0)),
                  _resident((D_MODEL, D_MODEL)),
                  pl.BlockSpec((1, N_MEM, 2 * D_MODEL), lambda i: (i // per_b, 0, 0)),
                  _resident((D_MODEL, D_MODEL)),
                  _resident((1, D_MODEL)), _resident((1, D_MODEL))],
        out_specs=pl.BlockSpec((tm, D_MODEL), lambda i: (i, 0)),
        out_shape=jax.ShapeDtypeStruct((t, D_MODEL), F32),
        compiler_params=_cparams(("arbitrary",)),
        name="xattn_ln2",
    )(h1, w_xq, kv, w_xo, g, b)


def _router_kernel(h_ref, w_ref, b_ref, idx_ref, gate_ref):
    hh, hl = _split_bf16(h_ref[...])
    wh, wl = _split_bf16(w_ref[...])
    logits = _dot(hh, wh) + _dot(hh, wl) + _dot(hl, wh) + b_ref[...]
    lane = lax.broadcasted_iota(jnp.int32, logits.shape, 1)
    neg = jnp.float32(-jnp.inf)
    logits = jnp.where(lane < N_EXPERTS, logits, neg)
    idx_out = jnp.zeros(logits.shape, jnp.int32)
    tops = []
    for k in range(TOP_K):
        m = jnp.max(logits, axis=-1, keepdims=True)
        idx = jnp.min(jnp.where(logits == m, lane, LANES), axis=-1, keepdims=True)
        idx_out = jnp.where(lane == k, idx, idx_out)
        logits = jnp.where(lane == idx, neg, logits)
        tops.append(m)
    ex = [jnp.exp(m - tops[0]) for m in tops]
    den = ex[0] + ex[1] + ex[2] + ex[3]
    gate_out = jnp.zeros(logits.shape, F32)
    for k in range(TOP_K):
        gate_out = jnp.where(lane == k, ex[k] / den, gate_out)
    idx_ref[...] = idx_out
    gate_ref[...] = gate_out


def _router(h2, w_rp, b_rp, tm):
    t = h2.shape[0]
    tm = min(tm, t)
    return pl.pallas_call(
        _router_kernel,
        grid=(t // tm,),
        in_specs=[pl.BlockSpec((tm, D_MODEL), lambda i: (i, 0)),
                  _resident((D_MODEL, LANES)), _resident((1, LANES))],
        out_specs=[pl.BlockSpec((tm, LANES), lambda i: (i, 0)),
                   pl.BlockSpec((tm, LANES), lambda i: (i, 0))],
        out_shape=[jax.ShapeDtypeStruct((t, LANES), jnp.int32),
                   jax.ShapeDtypeStruct((t, LANES), F32)],
        compiler_params=_cparams(("arbitrary",)),
        name="router_top4",
    )(h2, w_rp, b_rp)


DISPATCH_ROWS = 512
ISSUE_UNROLL = 8


def _dispatch_kernel(ns_ref, tok_ref, h_hbm, xs_hbm, sem):
    i = pl.program_id(0)
    n_active = ns_ref[0]

    def chunk_wait(slot):
        pltpu.make_async_copy(h_hbm.at[pl.ds(0, DISPATCH_ROWS), :],
                              xs_hbm.at[pl.ds(0, DISPATCH_ROWS), :], sem.at[slot]).wait()

    @pl.when(i < n_active)
    def _():
        base = i * DISPATCH_ROWS
        slot = i % 2

        def issue(j8, c):
            for u in range(ISSUE_UNROLL):
                j = j8 * ISSUE_UNROLL + u
                pltpu.make_async_copy(h_hbm.at[pl.ds(tok_ref[0, 0, j], 1), :],
                                      xs_hbm.at[pl.ds(base + j, 1), :], sem.at[slot]).start()
            return c
        lax.fori_loop(0, DISPATCH_ROWS // ISSUE_UNROLL, issue, 0)

    @pl.when(jnp.logical_and(i >= 1, i <= n_active))
    def _():
        chunk_wait((i + 1) % 2)


def _dispatch(h2, row_tok3, n_steps_active):
    n_steps = row_tok3.shape[0]
    grid_spec = pltpu.PrefetchScalarGridSpec(
        num_scalar_prefetch=1,
        grid=(n_steps + 1,),
        in_specs=[pl.BlockSpec((1, 1, DISPATCH_ROWS), lambda i, ns: (jnp.minimum(i, n_steps - 1), 0, 0),
                               memory_space=pltpu.SMEM),
                  pl.BlockSpec(memory_space=pl.ANY)],
        out_specs=pl.BlockSpec(memory_space=pl.ANY),
        scratch_shapes=[pltpu.SemaphoreType.DMA((2,))],
    )
    return pl.pallas_call(
        _dispatch_kernel,
        grid_spec=grid_spec,
        out_shape=jax.ShapeDtypeStruct((n_steps * DISPATCH_ROWS, D_MODEL), F32),
        compiler_params=_cparams(("arbitrary",)),
        name="moe_dispatch",
    )(n_steps_active, row_tok3, h2)


MOE_ROWS = 1024
MOE_SUB = 256
MOE_TF = 256
MOE_NF = D_EXPERT // MOE_TF
MOE_NN = D_MODEL // MOE_TF


def _moe_kernel(be_ref, nv_ref, nu_ref, x_ref, wgu_ref, bgu_ref, wd_ref, bd_ref, o_ref,
                xb_ref, act_ref, wgb_ref, wdb_ref):
    blk = pl.program_id(0)
    s = pl.program_id(1)
    n_valid = nv_ref[blk]
    n_sub = x_ref.shape[0] // MOE_SUB

    @pl.when(jnp.logical_and(s == 0, n_valid > 0))
    def _():
        xb_ref[...] = x_ref[...].astype(BF16)

    @pl.when(jnp.logical_and(s < MOE_NF, n_valid > 0))
    def _():
        wgb_ref[...] = wgu_ref[0].astype(BF16)
        w2 = 2 * MOE_TF
        r = lax.broadcasted_iota(jnp.int32, (w2, MOE_TF), 0)
        c = lax.broadcasted_iota(jnp.int32, (w2, MOE_TF), 1)
        sel = jnp.where(r == 2 * c, 1.0, 0.0).astype(BF16)
        for t in range(n_sub):
            @pl.when(t * MOE_SUB < n_valid)
            def _():
                rows = slice(t * MOE_SUB, (t + 1) * MOE_SUB)
                gu = _dot(xb_ref[rows, :], wgb_ref[...]) + bgu_ref[0]
                glu = jnp.minimum(gu, SWIGLU_LIMIT)
                a = glu * _sigmoid(SWIGLU_ALPHA * glu)
                lin = jnp.clip(gu, -SWIGLU_LIMIT, SWIGLU_LIMIT) + 1.0
                prod = a * pltpu.roll(lin, w2 - 1, 1)
                act_ref[s, rows, :] = _dot(prod.astype(BF16), sel).astype(BF16)

    @pl.when(s >= MOE_NF)
    def _():
        @pl.when(n_valid > 0)
        def _():
            wdb_ref[...] = wd_ref[0].astype(BF16)

        for t in range(n_sub):
            rows = slice(t * MOE_SUB, (t + 1) * MOE_SUB)

            @pl.when(t * MOE_SUB < n_valid)
            def _():
                act = jnp.concatenate([act_ref[f, rows, :] for f in range(MOE_NF)], axis=1)
                o_ref[rows, :] = _dot(act, wdb_ref[...]) + bd_ref[0]

            @pl.when(t * MOE_SUB >= n_valid)
            def _():
                o_ref[rows, :] = jnp.zeros((MOE_SUB, MOE_TF), F32)


def _moe_ffn(x_sorted, block_e, n_valid, n_used, w_gate_up, b_gate_up3, w_down, b_down3):
    rows = MOE_ROWS
    n_blocks = x_sorted.shape[0] // rows
    down = lambda s: jnp.maximum(s - MOE_NF, 0)
    grid_spec = pltpu.PrefetchScalarGridSpec(
        num_scalar_prefetch=3,
        grid=(n_blocks, MOE_NF + MOE_NN),
        in_specs=[
            pl.BlockSpec((rows, D_MODEL), lambda b, s, be, nv, nu: (jnp.minimum(b, nu[0] - 1), 0)),
            pl.BlockSpec((1, D_MODEL, 2 * MOE_TF),
                         lambda b, s, be, nv, nu: (be[b], 0, jnp.minimum(s, MOE_NF - 1))),
            pl.BlockSpec((1, 1, 2 * MOE_TF),
                         lambda b, s, be, nv, nu: (be[b], 0, jnp.minimum(s, MOE_NF - 1))),
            pl.BlockSpec((1, D_EXPERT, MOE_TF), lambda b, s, be, nv, nu: (be[b], 0, down(s))),
            pl.BlockSpec((1, 1, MOE_TF), lambda b, s, be, nv, nu: (be[b], 0, down(s))),
        ],
        out_specs=pl.BlockSpec((rows, MOE_TF), lambda b, s, be, nv, nu: (b, down(s))),
        scratch_shapes=[pltpu.VMEM((rows, D_MODEL), BF16),
                        pltpu.VMEM((MOE_NF, rows, MOE_TF), BF16),
                        pltpu.VMEM((D_MODEL, 2 * MOE_TF), BF16),
                        pltpu.VMEM((D_EXPERT, MOE_TF), BF16)],
    )
    return pl.pallas_call(
        _moe_kernel,
        grid_spec=grid_spec,
        out_shape=jax.ShapeDtypeStruct((n_blocks * rows, D_MODEL), F32),
        compiler_params=_cparams(("arbitrary", "arbitrary")),
        name="moe_ffn",
    )(block_e, n_valid, n_used, x_sorted, w_gate_up, b_gate_up3, w_down, b_down3)


COMBINE_TM = 256


def _combine_kernel(dcur_ref, dnxt_ref, y_hbm, h_ref, gate_ref, g_ref, b_ref, o_ref, yb_ref, sem):
    i = pl.program_id(0)
    n_tiles = pl.num_programs(0)
    tm = h_ref.shape[0]
    n = TOP_K * tm

    def issue_tile(d_ref, slot):
        def issue(j8, c):
            for u in range(ISSUE_UNROLL):
                j = j8 * ISSUE_UNROLL + u
                pltpu.make_async_copy(y_hbm.at[pl.ds(d_ref[0, 0, j], 1), :],
                                      yb_ref.at[slot, pl.ds(j, 1), :], sem.at[slot]).start()
            return c
        lax.fori_loop(0, n // ISSUE_UNROLL, issue, 0)

    @pl.when(i == 0)
    def _():
        issue_tile(dcur_ref, 0)

    @pl.when(i + 1 < n_tiles)
    def _():
        issue_tile(dnxt_ref, (i + 1) % 2)

    slot = i % 2
    pltpu.make_async_copy(y_hbm.at[pl.ds(0, n), :], yb_ref.at[slot], sem.at[slot]).wait()
    gate = gate_ref[...]
    acc = DEEPNORM_ALPHA * h_ref[...]
    for k in range(TOP_K):
        acc = acc + gate[:, k:k + 1] * yb_ref[slot, k * tm:(k + 1) * tm, :]
    o_ref[...] = _layer_norm(acc, g_ref[...], b_ref[...])


def _combine(dest3, y_sorted, h2, gates, g, b, tm):
    t = h2.shape[0]
    n_tiles = t // tm
    return pl.pallas_call(
        _combine_kernel,
        grid=(n_tiles,),
        in_specs=[pl.BlockSpec((1, 1, TOP_K * tm), lambda i: (i, 0, 0), memory_space=pltpu.SMEM),
                  pl.BlockSpec((1, 1, TOP_K * tm), lambda i: (jnp.minimum(i + 1, n_tiles - 1), 0, 0),
                               memory_space=pltpu.SMEM),
                  pl.BlockSpec(memory_space=pl.ANY),
                  pl.BlockSpec((tm, D_MODEL), lambda i: (i, 0)),
                  pl.BlockSpec((tm, LANES), lambda i: (i, 0)),
                  _resident((1, D_MODEL)), _resident((1, D_MODEL))],
        out_specs=pl.BlockSpec((tm, D_MODEL), lambda i: (i, 0)),
        out_shape=jax.ShapeDtypeStruct((t, D_MODEL), F32),
        scratch_shapes=[pltpu.VMEM((2, TOP_K * tm, D_MODEL), F32), pltpu.SemaphoreType.DMA((2,))],
        compiler_params=_cparams(("arbitrary",)),
        name="moe_combine_ln3",
    )(dest3, dest3, y_sorted, h2, gates, g, b)


def _dispatch_plan(top_e, n_tok):
    rows = MOE_ROWS
    n_assign = n_tok * TOP_K
    e_flat = top_e.reshape(n_assign)
    onehot = (e_flat[:, None] == jnp.arange(N_EXPERTS, dtype=jnp.int32)[None, :]).astype(jnp.int32)
    csum = jnp.cumsum(onehot, axis=0)
    pos = jnp.sum(csum * onehot, axis=1) - 1
    counts = csum[-1]
    padded = (counts + rows - 1) // rows * rows
    pend = jnp.cumsum(padded)
    pstart = pend - padded
    dest = (pstart[e_flat] + pos).astype(jnp.int32)
    n_blocks = -(-n_assign // rows) + N_EXPERTS
    tok_flat = jnp.arange(n_assign, dtype=jnp.int32) // TOP_K
    row_tok = jnp.zeros((n_blocks * rows,), jnp.int32).at[dest].set(tok_flat)
    n_used = (pend[-1] // rows).astype(jnp.int32)
    blk_idx = jnp.arange(n_blocks, dtype=jnp.int32)
    blk_start = blk_idx * rows
    block_e = jnp.minimum(jnp.sum((pend[None, :] <= blk_start[:, None]).astype(jnp.int32), axis=1),
                          N_EXPERTS - 1)
    n_valid = jnp.clip(counts[block_e] - (blk_start - pstart[block_e]), 0, rows)
    n_valid = jnp.where(blk_idx < n_used, n_valid, 0).astype(jnp.int32)
    block_e = jnp.where(blk_idx < n_used, block_e, block_e[n_used - 1]).astype(jnp.int32)
    n_steps = n_blocks * rows // DISPATCH_ROWS
    n_steps_active = (n_used * (rows // DISPATCH_ROWS)).astype(jnp.int32)
    return (dest, row_tok.reshape(n_steps, 1, DISPATCH_ROWS), n_steps_active.reshape(1),
            block_e, n_valid, n_used.reshape(1))


def kernel(x, mem, w_in, b_in, w_gla_a2, b_gla_a, gla_norm_g, hgrn_norm_g, hgrn_lb_logits, w_mix_o,
           w_xq, w_mem_kv, w_xo, w_router, b_router, w_gate_up, b_gate_up, w_down, b_down,
           ln1_g, ln1_b, ln2_g, ln2_b, ln3_g, ln3_b):
    batch, seq, d = x.shape
    t = batch * seq
    x2 = x.reshape(t, d)
    l = 0

    pad = D_IN_PAD - OFF_A1 - GLA_RANK
    w = w_in[l]
    w_in_p = jnp.concatenate([w[:, :GA1_SRC], w[:, GA1_SRC + GLA_RANK:], w[:, GA1_SRC:GA1_SRC + GLA_RANK],
                              jnp.zeros((d, pad), w.dtype)], axis=1).astype(BF16)
    bi = b_in[l]
    b_in_p = jnp.concatenate([bi[:GA1_SRC], bi[GA1_SRC + GLA_RANK:], bi[GA1_SRC:GA1_SRC + GLA_RANK],
                              jnp.zeros((pad,), bi.dtype)])[None, :]
    proj = _matmul_bias(x2, w_in_p, b_in_p, 1024, PROJ_TN, BF16)

    wa2p = jnp.concatenate([w_gla_a2[l], jnp.zeros((LANES - GLA_RANK, GLA_KEY), F32)], axis=0).astype(BF16)
    blk = min(512, seq)
    gla_out = _gla_branch(proj, wa2p, b_gla_a[l][None, :], gla_norm_g[l][None, :], batch, seq, blk)
    merged = _hgrn_branch(proj, gla_out, hgrn_lb_logits, hgrn_norm_g[l][None, :], batch, seq, blk)

    h1 = _out_ln(merged, w_mix_o[l].astype(BF16), x2, ln1_g[l][None, :], ln1_b[l][None, :], 512)

    kv = _matmul_bias(mem.reshape(batch * N_MEM, d), w_mem_kv[l].astype(BF16),
                      jnp.zeros((1, 2 * d), F32), 512, 512, BF16).reshape(batch, N_MEM, 2 * d)
    h2 = _xattn(h1, w_xq[l].astype(BF16), kv, w_xo[l].astype(BF16),
                ln2_g[l][None, :], ln2_b[l][None, :], seq, 512)

    w_rp = jnp.concatenate([w_router[l], jnp.zeros((d, LANES - N_EXPERTS), F32)], axis=1)
    b_rp = jnp.concatenate([b_router[l], jnp.zeros((LANES - N_EXPERTS,), F32)])[None, :]
    idx, gates = _router(h2, w_rp, b_rp, 512)

    dest, row_tok3, n_steps_active, block_e, n_valid, n_used = _dispatch_plan(idx[:, :TOP_K], t)
    x_sorted = _dispatch(h2, row_tok3, n_steps_active)
    y_sorted = _moe_ffn(x_sorted, block_e, n_valid, n_used, w_gate_up[l],
                        b_gate_up[l][:, None, :], w_down[l], b_down[l][:, None, :])

    tm_c = min(COMBINE_TM, t)
    dest3 = dest.reshape(t // tm_c, tm_c, TOP_K).transpose(0, 2, 1).reshape(t // tm_c, 1, TOP_K * tm_c)
    out = _combine(dest3, y_sorted, h2, gates, ln3_g[l][None, :], ln3_b[l][None, :], tm_c)
    return out.reshape(batch, seq, d)
```

```python
import jax
import jax.numpy as jnp
from jax import lax
from jax.experimental import pallas as pl
from jax.experimental.pallas import tpu as pltpu

F32 = jnp.float32
BF16 = jnp.bfloat16

D_MODEL = 2048
N_MEM = 256
GLA_HEADS = 4
GLA_KEY = D_MODEL // 2
GLA_VAL = D_MODEL
GLA_DK = GLA_KEY // GLA_HEADS
GLA_DV = GLA_VAL // GLA_HEADS
GLA_RANK = 16
GLA_TAU = 16.0
HGRN_DK = 128
HGRN_HEADS = D_MODEL // HGRN_DK
HGRN_DV = D_MODEL // HGRN_HEADS
X_HEADS = 4
X_DH = D_MODEL // X_HEADS
N_EXPERTS = 32
TOP_K = 4
D_EXPERT = D_MODEL
SWIGLU_LIMIT = 7.0
SWIGLU_ALPHA = 1.702
DEPTH = 1
DEEPNORM_ALPHA = (2.0 * DEPTH) ** 0.25
NORM_EPS = 1e-5

LANES = 128
CHUNK = 64
SUB = 16
N_SUB = CHUNK // SUB

OFF_GQ = 0
OFF_GK = GLA_KEY
OFF_GV = 2 * GLA_KEY
OFF_GR = OFF_GV + GLA_VAL
OFF_HQ = OFF_GR + GLA_VAL
OFF_HF = OFF_HQ + D_MODEL
OFF_HI = OFF_HF + D_MODEL
OFF_HG = OFF_HI + D_MODEL
OFF_MA = OFF_HG + D_MODEL
OFF_MB = OFF_MA + D_MODEL
OFF_A1 = OFF_MB + D_MODEL
PROJ_TN = 512
D_IN_PAD = -(-(OFF_A1 + LANES) // PROJ_TN) * PROJ_TN
GA1_SRC = OFF_GR + GLA_VAL

VMEM_LIMIT = 56 * 1024 * 1024


def _cparams(sem):
    return pltpu.CompilerParams(dimension_semantics=sem, vmem_limit_bytes=VMEM_LIMIT)


def _dot(a, b):
    return jnp.dot(a, b, preferred_element_type=F32)


def _dot_nt(a, b):
    return lax.dot_general(a, b, (((1,), (1,)), ((), ())), preferred_element_type=F32)


def _dot_tn(a, b):
    return lax.dot_general(a, b, (((0,), (0,)), ((), ())), preferred_element_type=F32)


def _sigmoid(x):
    return 1.0 / (1.0 + jnp.exp(-x))


def _log_sigmoid(x):
    return jnp.minimum(x, 0.0) - jnp.log1p(jnp.exp(-jnp.abs(x)))


def _layer_norm(r, g, b):
    mu = jnp.mean(r, axis=-1, keepdims=True)
    c = r - mu
    var = jnp.mean(c * c, axis=-1, keepdims=True)
    return c * lax.rsqrt(var + NORM_EPS) * g + b


def _split_bf16(x):
    hi = x.astype(BF16)
    lo = (x - hi.astype(F32)).astype(BF16)
    return hi, lo


def _proj_kernel(x_ref, w_ref, b_ref, o_ref, xb_ref):
    @pl.when(pl.program_id(1) == 0)
    def _():
        xb_ref[...] = x_ref[...].astype(BF16)

    o_ref[...] = (_dot(xb_ref[...], w_ref[...]) + b_ref[...]).astype(o_ref.dtype)


def _matmul_bias(x, w, b, tm, tn, out_dtype):
    m, k = x.shape
    n = w.shape[1]
    tm = min(tm, m)
    return pl.pallas_call(
        _proj_kernel,
        grid=(m // tm, n // tn),
        in_specs=[pl.BlockSpec((tm, k), lambda i, j: (i, 0)),
                  pl.BlockSpec((k, tn), lambda i, j: (0, j)),
                  pl.BlockSpec((1, tn), lambda i, j: (0, j))],
        out_specs=pl.BlockSpec((tm, tn), lambda i, j: (i, j)),
        out_shape=jax.ShapeDtypeStruct((m, n), out_dtype),
        scratch_shapes=[pltpu.VMEM((tm, k), BF16)],
        compiler_params=_cparams(("arbitrary", "arbitrary")),
        name="matmul_bias",
    )(x, w, b)


def _cum_matrix():
    r = lax.broadcasted_iota(jnp.int32, (2 * CHUNK, CHUNK), 0)
    c = lax.broadcasted_iota(jnp.int32, (2 * CHUNK, CHUNK), 1)
    incl = c <= r
    rr = r - CHUNK
    start = c < (rr // SUB) * SUB
    one = jnp.logical_or(jnp.logical_and(r < CHUNK, incl), jnp.logical_and(r >= CHUNK, start))
    return jnp.where(one, 1.0, 0.0).astype(BF16)


def _chunk_rows(x, r):
    n_chunks = x.shape[0] // CHUNK
    return jnp.concatenate(
        [jnp.broadcast_to(x[c * CHUNK + r:c * CHUNK + r + 1, :], (CHUNK, x.shape[1]))
         for c in range(n_chunks)], axis=0)


def _scan_block(q, k, v, lg, states, n_heads):
    rows, dkt = q.shape
    dk = dkt // n_heads
    dv = v.shape[1] // n_heads
    n_chunks = rows // CHUNK
    cmat = _cum_matrix()
    hi, lo = _split_bf16(lg)
    cums, csts = [], []
    for c in range(n_chunks):
        rs = slice(c * CHUNK, (c + 1) * CHUNK)
        c2 = _dot(cmat, hi[rs]) + _dot(cmat, lo[rs])
        cums.append(c2[:CHUNK])
        csts.append(c2[CHUNK:])
    cum = jnp.concatenate(cums, axis=0)
    cst = jnp.concatenate(csts, axis=0)
    last = _chunk_rows(cum, CHUNK - 1)
    qe = q * jnp.exp(cum - cst)
    qd = (qe * jnp.exp(cst)).astype(BF16)
    kd = (k * jnp.exp(last - cum)).astype(BF16)
    row_in = lax.broadcasted_iota(jnp.int32, (rows, dkt), 0) & (CHUNK - 1)
    q_sub, r_sub = [], []
    for i in range(N_SUB):
        c_i = _chunk_rows(cst, SUB * i)
        r_sub.append(jnp.where(row_in < SUB * (i + 1), k * jnp.exp(c_i - cum), 0.0).astype(BF16))
        q_sub.append(jnp.where((row_in >> 4) == i, qe, 0.0).astype(BF16))
    causal = (lax.broadcasted_iota(jnp.int32, (CHUNK, CHUNK), 1)
              <= lax.broadcasted_iota(jnp.int32, (CHUNK, CHUNK), 0))
    intra = [[None] * n_heads for _ in range(n_chunks)]
    upd = [[None] * n_heads for _ in range(n_chunks)]
    for h in range(n_heads):
        ks = slice(h * dk, (h + 1) * dk)
        vs = slice(h * dv, (h + 1) * dv)
        q_stack = jnp.concatenate([x[:, ks] for x in q_sub], axis=1)
        r_stack = jnp.concatenate([x[:, ks] for x in r_sub], axis=1)
        for c in range(n_chunks):
            rs = slice(c * CHUNK, (c + 1) * CHUNK)
            a = _dot_nt(q_stack[rs], r_stack[rs])
            a = jnp.where(causal, a, 0.0).astype(BF16)
            intra[c][h] = _dot(a, v[rs, vs])
            upd[c][h] = _dot_tn(v[rs, vs], kd[rs, ks])
    outs = [[None] * n_heads for _ in range(n_chunks)]
    new_states = []
    for h in range(n_heads):
        ks = slice(h * dk, (h + 1) * dk)
        s = states[h]
        for c in range(n_chunks):
            rs = slice(c * CHUNK, (c + 1) * CHUNK)
            outs[c][h] = _dot_nt(qd[rs, ks], s.astype(BF16)) + intra[c][h]
            e_last = jnp.exp(cum[(c + 1) * CHUNK - 1:(c + 1) * CHUNK, ks])
            s = s * e_last + upd[c][h]
        new_states.append(s)
    o = jnp.concatenate([jnp.concatenate(row, axis=1) for row in outs], axis=0)
    return o, new_states


def _group_rmsnorm(o, g, n_heads):
    dv = o.shape[1] // n_heads
    parts = []
    for h in range(n_heads):
        oh = o[:, h * dv:(h + 1) * dv]
        parts.append(oh * lax.rsqrt(jnp.mean(oh * oh, axis=-1, keepdims=True) + NORM_EPS) * g)
    return jnp.concatenate(parts, axis=1)


def _gla_kernel(q_ref, k_ref, v_ref, r_ref, ma_ref, a1_ref, wa2_ref, ba_ref, g_ref, o_ref, s_ref):
    @pl.when(pl.program_id(2) == 0)
    def _():
        s_ref[...] = jnp.zeros_like(s_ref)

    z = _dot(a1_ref[...], wa2_ref[...]) + ba_ref[...]
    lg = _log_sigmoid(z) * (1.0 / GLA_TAU)
    q = q_ref[...].astype(F32) * (GLA_DK ** -0.5)
    k = k_ref[...].astype(F32)
    o, new_states = _scan_block(q, k, v_ref[...], lg, [s_ref[...]], 1)
    s_ref[...] = new_states[0]
    o = _group_rmsnorm(o, g_ref[...], 1)
    r = r_ref[...].astype(F32)
    gate = _sigmoid(ma_ref[...].astype(F32))
    o_ref[...] = (gate * (o * (r * _sigmoid(r)))).astype(o_ref.dtype)


def _gla_branch(proj, wa2p, ba, gla_g, batch, seq, blk):
    nb = seq // blk
    row = lambda b, h, c: b * nb + c
    kb, vb = GLA_DK, GLA_DV
    in_specs = [
        pl.BlockSpec((blk, kb), lambda b, h, c: (row(b, h, c), OFF_GQ // kb + h)),
        pl.BlockSpec((blk, kb), lambda b, h, c: (row(b, h, c), OFF_GK // kb + h)),
        pl.BlockSpec((blk, vb), lambda b, h, c: (row(b, h, c), OFF_GV // vb + h)),
        pl.BlockSpec((blk, vb), lambda b, h, c: (row(b, h, c), OFF_GR // vb + h)),
        pl.BlockSpec((blk, vb), lambda b, h, c: (row(b, h, c), OFF_MA // vb + h)),
        pl.BlockSpec((blk, LANES), lambda b, h, c: (row(b, h, c), OFF_A1 // LANES)),
        pl.BlockSpec((LANES, kb), lambda b, h, c: (0, h)),
        pl.BlockSpec((1, kb), lambda b, h, c: (0, h)),
        pl.BlockSpec((1, vb), lambda b, h, c: (0, 0)),
    ]
    return pl.pallas_call(
        _gla_kernel,
        grid=(batch, GLA_HEADS, nb),
        in_specs=in_specs,
        out_specs=pl.BlockSpec((blk, vb), lambda b, h, c: (row(b, h, c), h)),
        out_shape=jax.ShapeDtypeStruct((batch * seq, GLA_VAL), BF16),
        scratch_shapes=[pltpu.VMEM((vb, kb), F32)],
        compiler_params=_cparams(("arbitrary", "arbitrary", "arbitrary")),
        name="gla_scan",
    )(proj, proj, proj, proj, proj, proj, wa2p, ba, gla_g)


HGRN_HP = 2


def _hgrn_kernel(q_ref, f_ref, i_ref, g_ref, mb_ref, gla_ref, lbl_ref, ng_ref, o_ref, s_ref):
    @pl.when(pl.program_id(2) == 0)
    def _():
        s_ref[...] = jnp.zeros_like(s_ref)

    lbl = lbl_ref[...].astype(F32)
    e = jnp.exp(lbl - jnp.max(lbl, axis=0, keepdims=True))
    lb = e[0:1, :] / jnp.sum(e, axis=0, keepdims=True)

    hq = q_ref[...].astype(F32)
    hf = f_ref[...].astype(F32)
    q = hq * _sigmoid(hq)
    lg = jnp.log(lb + (1.0 - lb) * _sigmoid(hf))
    k = (1.0 - lb) * _sigmoid(-hf)
    o, new_states = _scan_block(q, k, i_ref[...], lg, [s_ref[h] for h in range(HGRN_HP)], HGRN_HP)
    for h in range(HGRN_HP):
        s_ref[h] = new_states[h]
    o = _group_rmsnorm(o, ng_ref[...], HGRN_HP)
    o = o * _sigmoid(g_ref[...].astype(F32))
    merged = gla_ref[...].astype(F32) + _sigmoid(mb_ref[...].astype(F32)) * o
    o_ref[...] = merged.astype(o_ref.dtype)


def _hgrn_branch(proj, gla_out, lb_logits, hgrn_g, batch, seq, blk):
    nb = seq // blk
    w = HGRN_HP * HGRN_DK
    row = lambda b, h, c: b * nb + c
    in_specs = [
        pl.BlockSpec((blk, w), lambda b, h, c: (row(b, h, c), OFF_HQ // w + h)),
        pl.BlockSpec((blk, w), lambda b, h, c: (row(b, h, c), OFF_HF // w + h)),
        pl.BlockSpec((blk, w), lambda b, h, c: (row(b, h, c), OFF_HI // w + h)),
        pl.BlockSpec((blk, w), lambda b, h, c: (row(b, h, c), OFF_HG // w + h)),
        pl.BlockSpec((blk, w), lambda b, h, c: (row(b, h, c), OFF_MB // w + h)),
        pl.BlockSpec((blk, w), lambda b, h, c: (row(b, h, c), h)),
        pl.BlockSpec((DEPTH + 1, w), lambda b, h, c: (0, h)),
        pl.BlockSpec((1, HGRN_DV), lambda b, h, c: (0, 0)),
    ]
    return pl.pallas_call(
        _hgrn_kernel,
        grid=(batch, HGRN_HEADS // HGRN_HP, nb),
        in_specs=in_specs,
        out_specs=pl.BlockSpec((blk, w), lambda b, h, c: (row(b, h, c), h)),
        out_shape=jax.ShapeDtypeStruct((batch * seq, D_MODEL), BF16),
        scratch_shapes=[pltpu.VMEM((HGRN_HP, HGRN_DV, HGRN_DK), F32)],
        compiler_params=_cparams(("arbitrary", "arbitrary", "arbitrary")),
        name="hgrn_scan",
    )(proj, proj, proj, proj, proj, gla_out, lb_logits, hgrn_g)


def _out_ln_kernel(m_ref, w_ref, x_ref, g_ref, b_ref, o_ref):
    y = _dot(m_ref[...], w_ref[...])
    o_ref[...] = _layer_norm(DEEPNORM_ALPHA * x_ref[...] + y, g_ref[...], b_ref[...])


def _resident(shape):
    return pl.BlockSpec(shape, lambda *_: (0,) * len(shape), pipeline_mode=pl.Buffered(1))


def _out_ln(merged, w_o, x2, g, b, tm):
    t = merged.shape[0]
    tm = min(tm, t)
    return pl.pallas_call(
        _out_ln_kernel,
        grid=(t // tm,),
        in_specs=[pl.BlockSpec((tm, D_MODEL), lambda i: (i, 0)),
                  _resident((D_MODEL, D_MODEL)),
                  pl.BlockSpec((tm, D_MODEL), lambda i: (i, 0)),
                  _resident((1, D_MODEL)), _resident((1, D_MODEL))],
        out_specs=pl.BlockSpec((tm, D_MODEL), lambda i: (i, 0)),
        out_shape=jax.ShapeDtypeStruct((t, D_MODEL), F32),
        compiler_params=_cparams(("arbitrary",)),
        name="wo_ln1",
    )(merged, w_o, x2, g, b)


def _xattn_kernel(h_ref, wq_ref, kv_ref, wo_ref, g_ref, b_ref, o_ref):
    h = h_ref[...]
    q = _dot(h.astype(BF16), wq_ref[...]).astype(BF16)
    outs = []
    for hh in range(X_HEADS):
        sl = slice(hh * X_DH, (hh + 1) * X_DH)
        kh = kv_ref[0, :, sl]
        vh = kv_ref[0, :, D_MODEL + hh * X_DH:D_MODEL + (hh + 1) * X_DH]
        s = _dot_nt(q[:, sl], kh) * (X_DH ** -0.5)
        p = jnp.exp(s - jnp.max(s, axis=-1, keepdims=True))
        l = jnp.sum(p, axis=-1, keepdims=True)
        outs.append((_dot(p.astype(BF16), vh) / l).astype(BF16))
    o = jnp.concatenate(outs, axis=1)
    y = _dot(o, wo_ref[...])
    o_ref[...] = _layer_norm(DEEPNORM_ALPHA * h + y, g_ref[...], b_ref[...])


def _xattn(h1, w_xq, kv, w_xo, g, b, seq, tm):
    t = h1.shape[0]
    tm = min(tm, seq)
    per_b = seq // tm
    return pl.pallas_call(
        _xattn_kernel,
        grid=(t // tm,),
        in_specs=[pl.BlockSpec((tm, D_MODEL), lambda i: (i, 0)),
                  _resident((D_MODEL, D_MODEL)),
                  pl.BlockSpec((1, N_MEM, 2 * D_MODEL), lambda i: (i // per_b, 0, 0)),
                  _resident((D_MODEL, D_MODEL)),
                  _resident((1, D_MODEL)), _resident((1, D_MODEL))],
        out_specs=pl.BlockSpec((tm, D_MODEL), lambda i: (i, 0)),
        out_shape=jax.ShapeDtypeStruct((t, D_MODEL), F32),
        compiler_params=_cparams(("arbitrary",)),
        name="xattn_ln2",
    )(h1, w_xq, kv, w_xo, g, b)


def _router_kernel(h_ref, w_ref, b_ref, idx_ref, gate_ref):
    hh, hl = _split_bf16(h_ref[...])
    wh, wl = _split_bf16(w_ref[...])
    logits = _dot(hh, wh) + _dot(hh, wl) + _dot(hl, wh) + b_ref[...]
    lane = lax.broadcasted_iota(jnp.int32, logits.shape, 1)
    neg = jnp.float32(-jnp.inf)
    logits = jnp.where(lane < N_EXPERTS, logits, neg)
    idx_out = jnp.zeros(logits.shape, jnp.int32)
    tops = []
    for k in range(TOP_K):
        m = jnp.max(logits, axis=-1, keepdims=True)
        idx = jnp.min(jnp.where(logits == m, lane, LANES), axis=-1, keepdims=True)
        idx_out = jnp.where(lane == k, idx, idx_out)
        logits = jnp.where(lane == idx, neg, logits)
        tops.append(m)
    ex = [jnp.exp(m - tops[0]) for m in tops]
    den = ex[0] + ex[1] + ex[2] + ex[3]
    gate_out = jnp.zeros(logits.shape, F32)
    for k in range(TOP_K):
        gate_out = jnp.where(lane == k, ex[k] / den, gate_out)
    idx_ref[...] = idx_out
    gate_ref[...] = gate_out


def _router(h2, w_rp, b_rp, tm):
    t = h2.shape[0]
    tm = min(tm, t)
    return pl.pallas_call(
        _router_kernel,
        grid=(t // tm,),
        in_specs=[pl.BlockSpec((tm, D_MODEL), lambda i: (i, 0)),
                  _resident((D_MODEL, LANES)), _resident((1, LANES))],
        out_specs=[pl.BlockSpec((tm, LANES), lambda i: (i, 0)),
                   pl.BlockSpec((tm, LANES), lambda i: (i, 0))],
        out_shape=[jax.ShapeDtypeStruct((t, LANES), jnp.int32),
                   jax.ShapeDtypeStruct((t, LANES), F32)],
        compiler_params=_cparams(("arbitrary",)),
        name="router_top4",
    )(h2, w_rp, b_rp)


DISPATCH_TM = 256
ISSUE_UNROLL = 8
MOE_ROWS = 1024
MOE_SUB = 256
PAD_SLOTS = N_EXPERTS * MOE_SUB


def _dispatch_kernel(dest_ref, pad_ref, h_ref, xs_hbm, stage_ref, zero_ref, sem, zsem):
    i = pl.program_id(0)
    n_tiles = pl.num_programs(0) - 1
    tm = h_ref.shape[0]
    n = TOP_K * tm

    @pl.when(i < n_tiles)
    def _():
        slot = i % 2
        stage_ref[slot] = h_ref[...]

        def issue(j8, c):
            for u in range(ISSUE_UNROLL):
                j = j8 * ISSUE_UNROLL + u
                tok = j8 * (ISSUE_UNROLL // TOP_K) + u // TOP_K
                pltpu.make_async_copy(stage_ref.at[slot, pl.ds(tok, 1), :],
                                      xs_hbm.at[pl.ds(dest_ref[0, 0, j], 1), :], sem.at[slot]).start()
            return c
        lax.fori_loop(0, n // ISSUE_UNROLL, issue, 0)

    @pl.when(i >= 1)
    def _():
        for _ in range(TOP_K):
            pltpu.make_async_copy(stage_ref.at[0, pl.ds(0, tm), :], xs_hbm.at[pl.ds(0, tm), :],
                                  sem.at[(i + 1) % 2]).wait()

    @pl.when(i == n_tiles)
    def _():
        zero_ref[...] = jnp.zeros_like(zero_ref)

        def pad_copy(j):
            return pltpu.make_async_copy(zero_ref.at[pl.ds(0, 1), :],
                                         xs_hbm.at[pl.ds(pad_ref[0, 0, j], 1), :], zsem)

        def issue(j, c):
            @pl.when(pad_ref[0, 0, j] >= 0)
            def _():
                pad_copy(j).start()
            return c
        lax.fori_loop(0, PAD_SLOTS, issue, 0)

        def drain(j, c):
            @pl.when(pad_ref[0, 0, j] >= 0)
            def _():
                pad_copy(j).wait()
            return c
        lax.fori_loop(0, PAD_SLOTS, drain, 0)


def _dispatch(h2, dest_tiles, pad_dest, n_rows):
    t = h2.shape[0]
    tm = min(DISPATCH_TM, t)
    n_tiles = t // tm
    last = n_tiles - 1
    return pl.pallas_call(
        _dispatch_kernel,
        grid=(n_tiles + 1,),
        in_specs=[pl.BlockSpec((1, 1, TOP_K * tm), lambda i: (jnp.minimum(i, last), 0, 0),
                               memory_space=pltpu.SMEM),
                  pl.BlockSpec((1, 1, PAD_SLOTS), lambda i: (0, 0, 0), memory_space=pltpu.SMEM),
                  pl.BlockSpec((tm, D_MODEL), lambda i: (jnp.minimum(i, last), 0))],
        out_specs=pl.BlockSpec(memory_space=pl.ANY),
        out_shape=jax.ShapeDtypeStruct((n_rows, D_MODEL), F32),
        scratch_shapes=[pltpu.VMEM((2, tm, D_MODEL), F32), pltpu.VMEM((8, D_MODEL), F32),
                        pltpu.SemaphoreType.DMA((2,)), pltpu.SemaphoreType.DMA(())],
        compiler_params=_cparams(("arbitrary",)),
        name="moe_dispatch",
    )(dest_tiles, pad_dest, h2)


PREP_TK = 512
PREP_CW = 512


def _wprep_kernel(wgu_ref, wd_ref, wg_o, wu_o, wd_o):
    r = lax.broadcasted_iota(jnp.int32, (PREP_CW, PREP_CW // 2), 0)
    c = lax.broadcasted_iota(jnp.int32, (PREP_CW, PREP_CW // 2), 1)
    sel_even = jnp.where(r == 2 * c, 1.0, 0.0).astype(BF16)
    sel_odd = jnp.where(r == 2 * c + 1, 1.0, 0.0).astype(BF16)
    half = PREP_CW // 2
    for j in range(wgu_ref.shape[2] // PREP_CW):
        chunk = wgu_ref[0, :, j * PREP_CW:(j + 1) * PREP_CW].astype(BF16)
        wg_o[0, :, j * half:(j + 1) * half] = _dot(chunk, sel_even).astype(BF16)
        wu_o[0, :, j * half:(j + 1) * half] = _dot(chunk, sel_odd).astype(BF16)
    wd_o[...] = wd_ref[...].astype(BF16)


def _wprep(w_gate_up, w_down):
    n_e, d, two_f = w_gate_up.shape
    f = two_f // 2
    spec = lambda width: pl.BlockSpec((1, PREP_TK, width), lambda e, k: (e, k, 0))
    return pl.pallas_call(
        _wprep_kernel,
        grid=(n_e, d // PREP_TK),
        in_specs=[spec(two_f), spec(w_down.shape[2])],
        out_specs=[spec(f), spec(f), spec(w_down.shape[2])],
        out_shape=[jax.ShapeDtypeStruct((n_e, d, f), BF16), jax.ShapeDtypeStruct((n_e, d, f), BF16),
                   jax.ShapeDtypeStruct(w_down.shape, BF16)],
        compiler_params=_cparams(("arbitrary", "arbitrary")),
        name="moe_weight_prep",
    )(w_gate_up, w_down)


MOE_TF = 512
MOE_NF = D_EXPERT // MOE_TF
MOE_NN = D_MODEL // MOE_TF


def _moe_kernel(be_ref, nv_ref, nu_ref, x_ref, wg_ref, wu_ref, bg_ref, bu_ref, wd_ref, bd_ref, o_ref,
                xb_ref, act_ref):
    blk = pl.program_id(0)
    s = pl.program_id(1)
    n_valid = nv_ref[blk]
    n_sub = x_ref.shape[0] // MOE_SUB

    for t in range(n_sub):
        rows = slice(t * MOE_SUB, (t + 1) * MOE_SUB)
        valid = t * MOE_SUB < n_valid

        @pl.when(jnp.logical_and(valid, s == 0))
        def _():
            xb_ref[rows, :] = x_ref[rows, :].astype(BF16)

        @pl.when(jnp.logical_and(valid, s < MOE_NF))
        def _():
            xs = xb_ref[rows, :]
            glu = jnp.minimum(_dot(xs, wg_ref[0]) + bg_ref[0], SWIGLU_LIMIT)
            lin = jnp.clip(_dot(xs, wu_ref[0]) + bu_ref[0], -SWIGLU_LIMIT, SWIGLU_LIMIT)
            act_ref[s, rows, :] = (glu * _sigmoid(SWIGLU_ALPHA * glu) * (lin + 1.0)).astype(BF16)

        @pl.when(jnp.logical_and(valid, s >= MOE_NF))
        def _():
            act = jnp.concatenate([act_ref[f, rows, :] for f in range(MOE_NF)], axis=1)
            o_ref[rows, :] = _dot(act, wd_ref[0]) + bd_ref[0]

        @pl.when(jnp.logical_and(jnp.logical_not(valid), s >= MOE_NF))
        def _():
            o_ref[rows, :] = jnp.zeros((MOE_SUB, MOE_TF), F32)


def _moe_ffn(x_sorted, block_e, n_valid, n_used, wg, wu, bg3, bu3, wd, bd3):
    rows = MOE_ROWS
    n_blocks = x_sorted.shape[0] // rows
    up = lambda s: jnp.minimum(s, MOE_NF - 1)
    down = lambda s: jnp.maximum(s - MOE_NF, 0)
    grid_spec = pltpu.PrefetchScalarGridSpec(
        num_scalar_prefetch=3,
        grid=(n_blocks, MOE_NF + MOE_NN),
        in_specs=[
            pl.BlockSpec((rows, D_MODEL), lambda b, s, be, nv, nu: (jnp.minimum(b, nu[0] - 1), 0)),
            pl.BlockSpec((1, D_MODEL, MOE_TF), lambda b, s, be, nv, nu: (be[b], 0, up(s))),
            pl.BlockSpec((1, D_MODEL, MOE_TF), lambda b, s, be, nv, nu: (be[b], 0, up(s))),
            pl.BlockSpec((1, 1, MOE_TF), lambda b, s, be, nv, nu: (be[b], 0, up(s))),
            pl.BlockSpec((1, 1, MOE_TF), lambda b, s, be, nv, nu: (be[b], 0, up(s))),
            pl.BlockSpec((1, D_EXPERT, MOE_TF), lambda b, s, be, nv, nu: (be[b], 0, down(s))),
            pl.BlockSpec((1, 1, MOE_TF), lambda b, s, be, nv, nu: (be[b], 0, down(s))),
        ],
        out_specs=pl.BlockSpec((rows, MOE_TF), lambda b, s, be, nv, nu: (b, down(s))),
        scratch_shapes=[pltpu.VMEM((rows, D_MODEL), BF16),
                        pltpu.VMEM((MOE_NF, rows, MOE_TF), BF16)],
    )
    return pl.pallas_call(
        _moe_kernel,
        grid_spec=grid_spec,
        out_shape=jax.ShapeDtypeStruct((n_blocks * rows, D_MODEL), F32),
        compiler_params=_cparams(("arbitrary", "arbitrary")),
        name="moe_ffn",
    )(block_e, n_valid, n_used, x_sorted, wg, wu, bg3, bu3, wd, bd3)


COMBINE_TM = 256


def _combine_kernel(dcur_ref, dnxt_ref, y_hbm, h_ref, gate_ref, g_ref, b_ref, o_ref, yb_ref, sem):
    i = pl.program_id(0)
    n_tiles = pl.num_programs(0)
    tm = h_ref.shape[0]
    n = TOP_K * tm

    def issue_tile(d_ref, slot):
        def issue(j8, c):
            for u in range(ISSUE_UNROLL):
                j = j8 * ISSUE_UNROLL + u
                pltpu.make_async_copy(y_hbm.at[pl.ds(d_ref[0, 0, j], 1), :],
                                      yb_ref.at[slot, pl.ds(j, 1), :], sem.at[slot]).start()
            return c
        lax.fori_loop(0, n // ISSUE_UNROLL, issue, 0)

    @pl.when(i == 0)
    def _():
        issue_tile(dcur_ref, 0)

    @pl.when(i + 1 < n_tiles)
    def _():
        issue_tile(dnxt_ref, (i + 1) % 2)

    slot = i % 2
    pltpu.make_async_copy(y_hbm.at[pl.ds(0, n), :], yb_ref.at[slot], sem.at[slot]).wait()
    gate = gate_ref[...]
    acc = DEEPNORM_ALPHA * h_ref[...]
    for k in range(TOP_K):
        acc = acc + gate[:, k:k + 1] * yb_ref[slot, k * tm:(k + 1) * tm, :]
    o_ref[...] = _layer_norm(acc, g_ref[...], b_ref[...])


def _combine(dest3, y_sorted, h2, gates, g, b, tm):
    t = h2.shape[0]
    n_tiles = t // tm
    return pl.pallas_call(
        _combine_kernel,
        grid=(n_tiles,),
        in_specs=[pl.BlockSpec((1, 1, TOP_K * tm), lambda i: (i, 0, 0), memory_space=pltpu.SMEM),
                  pl.BlockSpec((1, 1, TOP_K * tm), lambda i: (jnp.minimum(i + 1, n_tiles - 1), 0, 0),
                               memory_space=pltpu.SMEM),
                  pl.BlockSpec(memory_space=pl.ANY),
                  pl.BlockSpec((tm, D_MODEL), lambda i: (i, 0)),
                  pl.BlockSpec((tm, LANES), lambda i: (i, 0)),
                  _resident((1, D_MODEL)), _resident((1, D_MODEL))],
        out_specs=pl.BlockSpec((tm, D_MODEL), lambda i: (i, 0)),
        out_shape=jax.ShapeDtypeStruct((t, D_MODEL), F32),
        scratch_shapes=[pltpu.VMEM((2, TOP_K * tm, D_MODEL), F32), pltpu.SemaphoreType.DMA((2,))],
        compiler_params=_cparams(("arbitrary",)),
        name="moe_combine_ln3",
    )(dest3, dest3, y_sorted, h2, gates, g, b)


def _dispatch_plan(top_e, n_tok):
    rows = MOE_ROWS
    n_assign = n_tok * TOP_K
    e_flat = top_e.reshape(n_assign)
    onehot = (e_flat[:, None] == jnp.arange(N_EXPERTS, dtype=jnp.int32)[None, :]).astype(jnp.int32)
    csum = jnp.cumsum(onehot, axis=0)
    pos = jnp.sum(csum * onehot, axis=1) - 1
    counts = csum[-1]
    padded = (counts + rows - 1) // rows * rows
    pend = jnp.cumsum(padded)
    pstart = pend - padded
    dest = (pstart[e_flat] + pos).astype(jnp.int32)
    n_blocks = -(-n_assign // rows) + N_EXPERTS
    n_used = (pend[-1] // rows).astype(jnp.int32)
    blk_idx = jnp.arange(n_blocks, dtype=jnp.int32)
    blk_start = blk_idx * rows
    block_e = jnp.minimum(jnp.sum((pend[None, :] <= blk_start[:, None]).astype(jnp.int32), axis=1),
                          N_EXPERTS - 1)
    n_valid = jnp.clip(counts[block_e] - (blk_start - pstart[block_e]), 0, rows)
    n_valid = jnp.where(blk_idx < n_used, n_valid, 0).astype(jnp.int32)
    block_e = jnp.where(blk_idx < n_used, block_e, block_e[n_used - 1]).astype(jnp.int32)
    j = jnp.arange(MOE_SUB, dtype=jnp.int32)[None, :]
    sub_end = (counts + MOE_SUB - 1) // MOE_SUB * MOE_SUB
    pad_dest = jnp.where(counts[:, None] + j < sub_end[:, None], (pstart + counts)[:, None] + j, -1)
    pad_dest = pad_dest.astype(jnp.int32).reshape(1, 1, PAD_SLOTS)
    return dest, pad_dest, block_e, n_valid, n_used.reshape(1), n_blocks * rows


def kernel(x, mem, w_in, b_in, w_gla_a2, b_gla_a, gla_norm_g, hgrn_norm_g, hgrn_lb_logits, w_mix_o,
           w_xq, w_mem_kv, w_xo, w_router, b_router, w_gate_up, b_gate_up, w_down, b_down,
           ln1_g, ln1_b, ln2_g, ln2_b, ln3_g, ln3_b):
    batch, seq, d = x.shape
    t = batch * seq
    x2 = x.reshape(t, d)
    l = 0

    pad = D_IN_PAD - OFF_A1 - GLA_RANK
    w = w_in[l]
    w_in_p = jnp.concatenate([w[:, :GA1_SRC], w[:, GA1_SRC + GLA_RANK:], w[:, GA1_SRC:GA1_SRC + GLA_RANK],
                              jnp.zeros((d, pad), w.dtype)], axis=1).astype(BF16)
    bi = b_in[l]
    b_in_p = jnp.concatenate([bi[:GA1_SRC], bi[GA1_SRC + GLA_RANK:], bi[GA1_SRC:GA1_SRC + GLA_RANK],
                              jnp.zeros((pad,), bi.dtype)])[None, :]
    proj = _matmul_bias(x2, w_in_p, b_in_p, 1024, PROJ_TN, BF16)

    wa2p = jnp.concatenate([w_gla_a2[l], jnp.zeros((LANES - GLA_RANK, GLA_KEY), F32)], axis=0).astype(BF16)
    blk = min(512, seq)
    gla_out = _gla_branch(proj, wa2p, b_gla_a[l][None, :], gla_norm_g[l][None, :], batch, seq, blk)
    merged = _hgrn_branch(proj, gla_out, hgrn_lb_logits, hgrn_norm_g[l][None, :], batch, seq, blk)

    h1 = _out_ln(merged, w_mix_o[l].astype(BF16), x2, ln1_g[l][None, :], ln1_b[l][None, :], 512)

    kv = _matmul_bias(mem.reshape(batch * N_MEM, d), w_mem_kv[l].astype(BF16),
                      jnp.zeros((1, 2 * d), F32), 512, 512, BF16).reshape(batch, N_MEM, 2 * d)
    h2 = _xattn(h1, w_xq[l].astype(BF16), kv, w_xo[l].astype(BF16),
                ln2_g[l][None, :], ln2_b[l][None, :], seq, 512)

    w_rp = jnp.concatenate([w_router[l], jnp.zeros((d, LANES - N_EXPERTS), F32)], axis=1)
    b_rp = jnp.concatenate([b_router[l], jnp.zeros((LANES - N_EXPERTS,), F32)])[None, :]
    idx, gates = _router(h2, w_rp, b_rp, 512)

    dest, pad_dest, block_e, n_valid, n_used, n_rows = _dispatch_plan(idx[:, :TOP_K], t)
    tm_d = min(DISPATCH_TM, t)
    x_sorted = _dispatch(h2, dest.reshape(t // tm_d, 1, TOP_K * tm_d), pad_dest, n_rows)
    wg, wu, wd = _wprep(w_gate_up[l], w_down[l])
    y_sorted = _moe_ffn(x_sorted, block_e, n_valid, n_used, wg, wu,
                        b_gate_up[l][:, None, 0::2], b_gate_up[l][:, None, 1::2],
                        wd, b_down[l][:, None, :])

    tm_c = min(COMBINE_TM, t)
    dest3 = dest.reshape(t // tm_c, tm_c, TOP_K).transpose(0, 2, 1).reshape(t // tm_c, 1, TOP_K * tm_c)
    out = _combine(dest3, y_sorted, h2, gates, ln3_g[l][None, :], ln3_b[l][None, :], tm_c)
    return out.reshape(batch, seq, d)
```

```python
import jax
import jax.numpy as jnp
from jax import lax
from jax.experimental import pallas as pl
from jax.experimental.pallas import tpu as pltpu

F32 = jnp.float32
BF16 = jnp.bfloat16

D_MODEL = 2048
N_MEM = 256
GLA_HEADS = 4
GLA_KEY = D_MODEL // 2
GLA_VAL = D_MODEL
GLA_DK = GLA_KEY // GLA_HEADS
GLA_DV = GLA_VAL // GLA_HEADS
GLA_RANK = 16
GLA_TAU = 16.0
HGRN_DK = 128
HGRN_HEADS = D_MODEL // HGRN_DK
HGRN_DV = D_MODEL // HGRN_HEADS
X_HEADS = 4
X_DH = D_MODEL // X_HEADS
N_EXPERTS = 32
TOP_K = 4
D_EXPERT = D_MODEL
SWIGLU_LIMIT = 7.0
SWIGLU_ALPHA = 1.702
DEPTH = 1
DEEPNORM_ALPHA = (2.0 * DEPTH) ** 0.25
NORM_EPS = 1e-5

LANES = 128
CHUNK = 64
SUB = 16
N_SUB = CHUNK // SUB
SUB_SHIFT = SUB.bit_length() - 1

OFF_GQ = 0
OFF_GK = GLA_KEY
OFF_GV = 2 * GLA_KEY
OFF_GR = OFF_GV + GLA_VAL
OFF_HQ = OFF_GR + GLA_VAL
OFF_HF = OFF_HQ + D_MODEL
OFF_HI = OFF_HF + D_MODEL
OFF_HG = OFF_HI + D_MODEL
OFF_MA = OFF_HG + D_MODEL
OFF_MB = OFF_MA + D_MODEL
OFF_A1 = OFF_MB + D_MODEL
PROJ_TN = 512
D_IN_PAD = -(-(OFF_A1 + LANES) // PROJ_TN) * PROJ_TN
GA1_SRC = OFF_GR + GLA_VAL

VMEM_LIMIT = 56 * 1024 * 1024


def _cparams(sem):
    return pltpu.CompilerParams(dimension_semantics=sem, vmem_limit_bytes=VMEM_LIMIT)


def _dot(a, b):
    return jnp.dot(a, b, preferred_element_type=F32)


def _dot_nt(a, b):
    return lax.dot_general(a, b, (((1,), (1,)), ((), ())), preferred_element_type=F32)


def _dot_tn(a, b):
    return lax.dot_general(a, b, (((0,), (0,)), ((), ())), preferred_element_type=F32)


def _sigmoid(x):
    return 0.5 * jnp.tanh(0.5 * x) + 0.5


def _log_sigmoid(x):
    return jnp.minimum(x, 0.0) - jnp.log1p(jnp.exp(-jnp.abs(x)))


def _layer_norm(r, g, b):
    mu = jnp.mean(r, axis=-1, keepdims=True)
    c = r - mu
    var = jnp.mean(c * c, axis=-1, keepdims=True)
    return c * lax.rsqrt(var + NORM_EPS) * g + b


def _split_bf16(x):
    hi = x.astype(BF16)
    lo = (x - hi.astype(F32)).astype(BF16)
    return hi, lo


def _proj_kernel(x_ref, w_ref, b_ref, o_ref, xb_ref):
    @pl.when(pl.program_id(1) == 0)
    def _():
        xb_ref[...] = x_ref[...].astype(BF16)

    o_ref[...] = (_dot(xb_ref[...], w_ref[...]) + b_ref[...]).astype(o_ref.dtype)


def _matmul_bias(x, w, b, tm, tn, out_dtype):
    m, k = x.shape
    n = w.shape[1]
    tm = min(tm, m)
    return pl.pallas_call(
        _proj_kernel,
        grid=(m // tm, n // tn),
        in_specs=[pl.BlockSpec((tm, k), lambda i, j: (i, 0)),
                  pl.BlockSpec((k, tn), lambda i, j: (0, j)),
                  pl.BlockSpec((1, tn), lambda i, j: (0, j))],
        out_specs=pl.BlockSpec((tm, tn), lambda i, j: (i, j)),
        out_shape=jax.ShapeDtypeStruct((m, n), out_dtype),
        scratch_shapes=[pltpu.VMEM((tm, k), BF16)],
        compiler_params=_cparams(("arbitrary", "arbitrary")),
        name="matmul_bias",
    )(x, w, b)


PROJ_NG = OFF_HQ // PROJ_TN
PROJ_NH = (OFF_A1 - OFF_HQ) // PROJ_TN


def _in_proj_kernel(x_ref, wg_ref, wh_ref, wa_ref, b_ref, o_ref, xb_ref):
    j = pl.program_id(1)

    @pl.when(j == 0)
    def _():
        xb_ref[...] = x_ref[...].astype(BF16)

    def tile(w_ref):
        o_ref[...] = (_dot(xb_ref[...], w_ref[...]) + b_ref[...]).astype(o_ref.dtype)

    @pl.when(j < PROJ_NG)
    def _():
        tile(wg_ref)

    @pl.when(jnp.logical_and(j >= PROJ_NG, j < PROJ_NG + PROJ_NH))
    def _():
        tile(wh_ref)

    @pl.when(j >= PROJ_NG + PROJ_NH)
    def _():
        tile(wa_ref)


def _in_proj(x, wg, wh, wa, b, tm):
    m, k = x.shape
    tm = min(tm, m)
    tn = PROJ_TN
    return pl.pallas_call(
        _in_proj_kernel,
        grid=(m // tm, D_IN_PAD // tn),
        in_specs=[pl.BlockSpec((tm, k), lambda i, j: (i, 0)),
                  pl.BlockSpec((k, tn), lambda i, j: (0, jnp.minimum(j, PROJ_NG - 1))),
                  pl.BlockSpec((k, tn), lambda i, j: (0, jnp.clip(j - PROJ_NG, 0, PROJ_NH - 1))),
                  pl.BlockSpec((k, tn), lambda i, j: (0, 0)),
                  pl.BlockSpec((1, tn), lambda i, j: (0, j))],
        out_specs=pl.BlockSpec((tm, tn), lambda i, j: (i, j)),
        out_shape=jax.ShapeDtypeStruct((m, D_IN_PAD), BF16),
        scratch_shapes=[pltpu.VMEM((tm, k), BF16)],
        compiler_params=_cparams(("arbitrary", "arbitrary")),
        name="in_proj",
    )(x, wg, wh, wa, b)


CUM_ROWS = CHUNK + 16


def _cum_matrix():
    r = lax.broadcasted_iota(jnp.int32, (CUM_ROWS, CHUNK), 0)
    c = lax.broadcasted_iota(jnp.int32, (CUM_ROWS, CHUNK), 1)
    within = jnp.logical_and(c <= r, (c >> SUB_SHIFT) == (r >> SUB_SHIFT))
    bound = c < (r - CHUNK) * SUB
    one = jnp.logical_or(jnp.logical_and(r < CHUNK, within), jnp.logical_and(r >= CHUNK, bound))
    return jnp.where(one, 1.0, 0.0).astype(BF16)


def _scan_block(q, k, v, lg, states, n_heads):
    rows, dkt = q.shape
    dk = dkt // n_heads
    dv = v.shape[1] // n_heads
    n_chunks = rows // CHUNK
    cmat = _cum_matrix()
    hi, lo = _split_bf16(lg)
    us, bounds = [], []
    for c in range(n_chunks):
        c2 = _dot(cmat, hi[c * CHUNK:(c + 1) * CHUNK]) + _dot(cmat, lo[c * CHUNK:(c + 1) * CHUNK])
        us.append(c2[:CHUNK])
        bounds.append(c2[CHUNK:CHUNK + 8])
    u = jnp.concatenate(us, axis=0)
    c_sub = [b[:N_SUB] for b in bounds]
    c_all = [b[N_SUB:N_SUB + 1] for b in bounds]
    sub_id = lax.broadcasted_iota(jnp.int32, (N_SUB, dkt), 0)

    def spread(tables):
        return jnp.concatenate([jnp.broadcast_to(t[j:j + 1, :], (SUB, dkt))
                                for t in tables for j in range(N_SUB)], axis=0)

    qe = q * jnp.exp(u)
    kq = k * jnp.exp(-u)
    qd = (qe * spread([jnp.exp(t) for t in c_sub])).astype(BF16)
    kd = (kq * spread([jnp.exp(a - t) for a, t in zip(c_all, c_sub)])).astype(BF16)
    row_in = lax.broadcasted_iota(jnp.int32, (rows, dkt), 0) & (CHUNK - 1)
    q_sub, r_sub = [], []
    for i in range(N_SUB):
        fac = [jnp.where(sub_id <= i, jnp.exp(t[i:i + 1, :] - t), 0.0) for t in c_sub]
        r_sub.append((kq * spread(fac)).astype(BF16))
        q_sub.append(jnp.where((row_in >> SUB_SHIFT) == i, qe, 0.0).astype(BF16))
    causal = (lax.broadcasted_iota(jnp.int32, (CHUNK, CHUNK), 1)
              <= lax.broadcasted_iota(jnp.int32, (CHUNK, CHUNK), 0))
    pairs = [(c, h) for h in range(n_heads) for c in range(n_chunks)]
    rs = lambda c: slice(c * CHUNK, (c + 1) * CHUNK)
    ks = lambda h: slice(h * dk, (h + 1) * dk)
    vs = lambda h: slice(h * dv, (h + 1) * dv)
    q_stack = [jnp.concatenate([x[:, ks(h)] for x in q_sub], axis=1) for h in range(n_heads)]
    r_stack = [jnp.concatenate([x[:, ks(h)] for x in r_sub], axis=1) for h in range(n_heads)]
    upd = {p: _dot_tn(v[rs(p[0]), vs(p[1])], kd[rs(p[0]), ks(p[1])]) for p in pairs}
    a = {p: _dot_nt(q_stack[p[1]][rs(p[0])], r_stack[p[1]][rs(p[0])]) for p in pairs}
    a = {p: jnp.where(causal, a[p], 0.0).astype(BF16) for p in pairs}
    s_in = {}
    new_states = []
    for h in range(n_heads):
        s = states[h]
        for c in range(n_chunks):
            s_in[(c, h)] = s.astype(BF16)
            s = s * jnp.exp(c_all[c][:, ks(h)]) + upd[(c, h)]
        new_states.append(s)
    inter = {p: _dot_nt(qd[rs(p[0]), ks(p[1])], s_in[p]) for p in pairs}
    intra = {p: _dot(a[p], v[rs(p[0]), vs(p[1])]) for p in pairs}
    o = jnp.concatenate([jnp.concatenate([inter[(c, h)] + intra[(c, h)] for h in range(n_heads)], axis=1)
                         for c in range(n_chunks)], axis=0)
    return o, new_states


def _group_rmsnorm(o, g, n_heads):
    dv = o.shape[1] // n_heads
    parts = []
    for h in range(n_heads):
        oh = o[:, h * dv:(h + 1) * dv]
        parts.append(oh * lax.rsqrt(jnp.mean(oh * oh, axis=-1, keepdims=True) + NORM_EPS) * g)
    return jnp.concatenate(parts, axis=1)


def _gla_kernel(q_ref, k_ref, v_ref, r_ref, ma_ref, a1_ref, wa2_ref, ba_ref, g_ref, o_ref, s_ref):
    @pl.when(pl.program_id(2) == 0)
    def _():
        s_ref[...] = jnp.zeros_like(s_ref)

    z = _dot(a1_ref[...], wa2_ref[...]) + ba_ref[...]
    lg = _log_sigmoid(z) * (1.0 / GLA_TAU)
    q = q_ref[...].astype(F32) * (GLA_DK ** -0.5)
    k = k_ref[...].astype(F32)
    o, new_states = _scan_block(q, k, v_ref[...], lg, [s_ref[...]], 1)
    s_ref[...] = new_states[0]
    o = _group_rmsnorm(o, g_ref[...], 1)
    r = r_ref[...].astype(F32)
    gate = _sigmoid(ma_ref[...].astype(F32))
    o_ref[...] = (gate * (o * (r * _sigmoid(r)))).astype(o_ref.dtype)


def _gla_branch(proj, wa2p, ba, gla_g, batch, seq, blk):
    nb = seq // blk
    row = lambda b, h, c: b * nb + c
    kb, vb = GLA_DK, GLA_DV
    in_specs = [
        pl.BlockSpec((blk, kb), lambda b, h, c: (row(b, h, c), OFF_GQ // kb + h)),
        pl.BlockSpec((blk, kb), lambda b, h, c: (row(b, h, c), OFF_GK // kb + h)),
        pl.BlockSpec((blk, vb), lambda b, h, c: (row(b, h, c), OFF_GV // vb + h)),
        pl.BlockSpec((blk, vb), lambda b, h, c: (row(b, h, c), OFF_GR // vb + h)),
        pl.BlockSpec((blk, vb), lambda b, h, c: (row(b, h, c), OFF_MA // vb + h)),
        pl.BlockSpec((blk, LANES), lambda b, h, c: (row(b, h, c), OFF_A1 // LANES)),
        pl.BlockSpec((LANES, kb), lambda b, h, c: (0, h)),
        pl.BlockSpec((1, kb), lambda b, h, c: (0, h)),
        pl.BlockSpec((1, vb), lambda b, h, c: (0, 0)),
    ]
    return pl.pallas_call(
        _gla_kernel,
        grid=(batch, GLA_HEADS, nb),
        in_specs=in_specs,
        out_specs=pl.BlockSpec((blk, vb), lambda b, h, c: (row(b, h, c), h)),
        out_shape=jax.ShapeDtypeStruct((batch * seq, GLA_VAL), BF16),
        scratch_shapes=[pltpu.VMEM((vb, kb), F32)],
        compiler_params=_cparams(("arbitrary", "arbitrary", "arbitrary")),
        name="gla_scan",
    )(proj, proj, proj, proj, proj, proj, wa2p, ba, gla_g)


HGRN_HP = 2


def _hgrn_kernel(q_ref, f_ref, i_ref, g_ref, mb_ref, gla_ref, lbl_ref, ng_ref, o_ref, s_ref):
    @pl.when(pl.program_id(2) == 0)
    def _():
        s_ref[...] = jnp.zeros_like(s_ref)

    lbl = lbl_ref[...].astype(F32)
    e = jnp.exp(lbl - jnp.max(lbl, axis=0, keepdims=True))
    lb = e[0:1, :] / jnp.sum(e, axis=0, keepdims=True)

    hq = q_ref[...].astype(F32)
    hf = f_ref[...].astype(F32)
    q = hq * _sigmoid(hq)
    th = jnp.tanh(0.5 * hf)
    half = 0.5 * (1.0 - lb)
    lg = jnp.log(lb + half * (1.0 + th))
    k = half * (1.0 - th)
    o, new_states = _scan_block(q, k, i_ref[...], lg, [s_ref[h] for h in range(HGRN_HP)], HGRN_HP)
    for h in range(HGRN_HP):
        s_ref[h] = new_states[h]
    o = _group_rmsnorm(o, ng_ref[...], HGRN_HP)
    o = o * _sigmoid(g_ref[...].astype(F32))
    merged = gla_ref[...].astype(F32) + _sigmoid(mb_ref[...].astype(F32)) * o
    o_ref[...] = merged.astype(o_ref.dtype)


def _hgrn_branch(proj, gla_out, lb_logits, hgrn_g, batch, seq, blk):
    nb = seq // blk
    w = HGRN_HP * HGRN_DK
    row = lambda b, h, c: b * nb + c
    in_specs = [
        pl.BlockSpec((blk, w), lambda b, h, c: (row(b, h, c), OFF_HQ // w + h)),
        pl.BlockSpec((blk, w), lambda b, h, c: (row(b, h, c), OFF_HF // w + h)),
        pl.BlockSpec((blk, w), lambda b, h, c: (row(b, h, c), OFF_HI // w + h)),
        pl.BlockSpec((blk, w), lambda b, h, c: (row(b, h, c), OFF_HG // w + h)),
        pl.BlockSpec((blk, w), lambda b, h, c: (row(b, h, c), OFF_MB // w + h)),
        pl.BlockSpec((blk, w), lambda b, h, c: (row(b, h, c), h)),
        pl.BlockSpec((DEPTH + 1, w), lambda b, h, c: (0, h)),
        pl.BlockSpec((1, HGRN_DV), lambda b, h, c: (0, 0)),
    ]
    return pl.pallas_call(
        _hgrn_kernel,
        grid=(batch, HGRN_HEADS // HGRN_HP, nb),
        in_specs=in_specs,
        out_specs=pl.BlockSpec((blk, w), lambda b, h, c: (row(b, h, c), h)),
        out_shape=jax.ShapeDtypeStruct((batch * seq, D_MODEL), BF16),
        scratch_shapes=[pltpu.VMEM((HGRN_HP, HGRN_DV, HGRN_DK), F32)],
        compiler_params=_cparams(("arbitrary", "arbitrary", "arbitrary")),
        name="hgrn_scan",
    )(proj, proj, proj, proj, proj, gla_out, lb_logits, hgrn_g)


def _out_ln_kernel(m_ref, w_ref, x_ref, g_ref, b_ref, o_ref):
    y = _dot(m_ref[...], w_ref[...])
    o_ref[...] = _layer_norm(DEEPNORM_ALPHA * x_ref[...] + y, g_ref[...], b_ref[...])


def _resident(shape):
    return pl.BlockSpec(shape, lambda *_: (0,) * len(shape), pipeline_mode=pl.Buffered(1))


def _out_ln(merged, w_o, x2, g, b, tm):
    t = merged.shape[0]
    tm = min(tm, t)
    return pl.pallas_call(
        _out_ln_kernel,
        grid=(t // tm,),
        in_specs=[pl.BlockSpec((tm, D_MODEL), lambda i: (i, 0)),
                  _resident((D_MODEL, D_MODEL)),
                  pl.BlockSpec((tm, D_MODEL), lambda i: (i, 0)),
                  _resident((1, D_MODEL)), _resident((1, D_MODEL))],
        out_specs=pl.BlockSpec((tm, D_MODEL), lambda i: (i, 0)),
        out_shape=jax.ShapeDtypeStruct((t, D_MODEL), F32),
        compiler_params=_cparams(("arbitrary",)),
        name="wo_ln1",
    )(merged, w_o, x2, g, b)


def _xattn_kernel(h_ref, wq_ref, kv_ref, wo_ref, g_ref, b_ref, o_ref):
    h = h_ref[...]
    q = _dot(h.astype(BF16), wq_ref[...]).astype(BF16)
    outs = []
    for hh in range(X_HEADS):
        sl = slice(hh * X_DH, (hh + 1) * X_DH)
        kh = kv_ref[0, :, sl]
        vh = kv_ref[0, :, D_MODEL + hh * X_DH:D_MODEL + (hh + 1) * X_DH]
        s = _dot_nt(q[:, sl], kh) * (X_DH ** -0.5)
        p = jnp.exp(s - jnp.max(s, axis=-1, keepdims=True))
        l = jnp.sum(p, axis=-1, keepdims=True)
        outs.append((_dot(p.astype(BF16), vh) * (1.0 / l)).astype(BF16))
    o = jnp.concatenate(outs, axis=1)
    y = _dot(o, wo_ref[...])
    o_ref[...] = _layer_norm(DEEPNORM_ALPHA * h + y, g_ref[...], b_ref[...])


def _xattn(h1, w_xq, kv, w_xo, g, b, seq, tm):
    t = h1.shape[0]
    tm = min(tm, seq)
    per_b = seq // tm
    return pl.pallas_call(
        _xattn_kernel,
        grid=(t // tm,),
        in_specs=[pl.BlockSpec((tm, D_MODEL), lambda i: (i, 0)),
                  _resident((D_MODEL, D_MODEL)),
                  pl.BlockSpec((1, N_MEM, 2 * D_MODEL), lambda i: (i // per_b, 0, 0)),
                  _resident((D_MODEL, D_MODEL)),
                  _resident((1, D_MODEL)), _resident((1, D_MODEL))],
        out_specs=pl.BlockSpec((tm, D_MODEL), lambda i: (i, 0)),
        out_shape=jax.ShapeDtypeStruct((t, D_MODEL), F32),
        compiler_params=_cparams(("arbitrary",)),
        name="xattn_ln2",
    )(h1, w_xq, kv, w_xo, g, b)


def _router_kernel(h_ref, w_ref, b_ref, idx_ref, gate_ref):
    hh, hl = _split_bf16(h_ref[...])
    wh, wl = _split_bf16(w_ref[...])
    logits = _dot(hh, wh) + _dot(hh, wl) + _dot(hl, wh) + b_ref[...]
    lane = lax.broadcasted_iota(jnp.int32, logits.shape, 1)
    neg = jnp.float32(-jnp.inf)
    logits = jnp.where(lane < N_EXPERTS, logits, neg)
    idx_out = jnp.zeros(logits.shape, jnp.int32)
    tops = []
    for k in range(TOP_K):
        m = jnp.max(logits, axis=-1, keepdims=True)
        idx = jnp.min(jnp.where(logits == m, lane, LANES), axis=-1, keepdims=True)
        idx_out = jnp.where(lane == k, idx, idx_out)
        logits = jnp.where(lane == idx, neg, logits)
        tops.append(m)
    ex = [jnp.exp(m - tops[0]) for m in tops]
    den = ex[0] + ex[1] + ex[2] + ex[3]
    gate_out = jnp.zeros(logits.shape, F32)
    for k in range(TOP_K):
        gate_out = jnp.where(lane == k, ex[k] / den, gate_out)
    idx_ref[...] = idx_out
    gate_ref[...] = gate_out


def _router(h2, w_rp, b_rp, tm):
    t = h2.shape[0]
    tm = min(tm, t)
    return pl.pallas_call(
        _router_kernel,
        grid=(t // tm,),
        in_specs=[pl.BlockSpec((tm, D_MODEL), lambda i: (i, 0)),
                  _resident((D_MODEL, LANES)), _resident((1, LANES))],
        out_specs=[pl.BlockSpec((tm, LANES), lambda i: (i, 0)),
                   pl.BlockSpec((tm, LANES), lambda i: (i, 0))],
        out_shape=[jax.ShapeDtypeStruct((t, LANES), jnp.int32),
                   jax.ShapeDtypeStruct((t, LANES), F32)],
        compiler_params=_cparams(("arbitrary",)),
        name="router_top4",
    )(h2, w_rp, b_rp)


DISPATCH_TM = 256
ISSUE_UNROLL = 8
MOE_ROWS = 1024
MOE_SUB = 256
PAD_SLOTS = N_EXPERTS * MOE_SUB


def _dispatch_kernel(dest_ref, pad_ref, h_ref, xs_hbm, stage_ref, zero_ref, sem, zsem):
    i = pl.program_id(0)
    n_tiles = pl.num_programs(0) - 1
    tm = h_ref.shape[0]
    n = TOP_K * tm

    @pl.when(i < n_tiles)
    def _():
        slot = i % 2
        stage_ref[slot] = h_ref[...]

        def issue(j8, c):
            for u in range(ISSUE_UNROLL):
                j = j8 * ISSUE_UNROLL + u
                tok = j8 * (ISSUE_UNROLL // TOP_K) + u // TOP_K
                pltpu.make_async_copy(stage_ref.at[slot, pl.ds(tok, 1), :],
                                      xs_hbm.at[pl.ds(dest_ref[0, 0, j], 1), :],
                                      sem.at[slot]).start(priority=u % 2)
            return c
        lax.fori_loop(0, n // ISSUE_UNROLL, issue, 0)

    @pl.when(i >= 1)
    def _():
        for _ in range(TOP_K):
            pltpu.make_async_copy(stage_ref.at[0, pl.ds(0, tm), :], xs_hbm.at[pl.ds(0, tm), :],
                                  sem.at[(i + 1) % 2]).wait()

    @pl.when(i == n_tiles)
    def _():
        zero_ref[...] = jnp.zeros_like(zero_ref)

        def pad_copy(j):
            return pltpu.make_async_copy(zero_ref.at[pl.ds(0, 1), :],
                                         xs_hbm.at[pl.ds(pad_ref[0, 0, j], 1), :], zsem)

        def issue(j, c):
            @pl.when(pad_ref[0, 0, j] >= 0)
            def _():
                pad_copy(j).start()
            return c
        lax.fori_loop(0, PAD_SLOTS, issue, 0)

        def drain(j, c):
            @pl.when(pad_ref[0, 0, j] >= 0)
            def _():
                pad_copy(j).wait()
            return c
        lax.fori_loop(0, PAD_SLOTS, drain, 0)


def _dispatch(h2, dest_tiles, pad_dest, n_rows):
    t = h2.shape[0]
    tm = min(DISPATCH_TM, t)
    n_tiles = t // tm
    last = n_tiles - 1
    return pl.pallas_call(
        _dispatch_kernel,
        grid=(n_tiles + 1,),
        in_specs=[pl.BlockSpec((1, 1, TOP_K * tm), lambda i: (jnp.minimum(i, last), 0, 0),
                               memory_space=pltpu.SMEM),
                  pl.BlockSpec((1, 1, PAD_SLOTS), lambda i: (0, 0, 0), memory_space=pltpu.SMEM),
                  pl.BlockSpec((tm, D_MODEL), lambda i: (jnp.minimum(i, last), 0))],
        out_specs=pl.BlockSpec(memory_space=pl.ANY),
        out_shape=jax.ShapeDtypeStruct((n_rows, D_MODEL), F32),
        scratch_shapes=[pltpu.VMEM((2, tm, D_MODEL), F32), pltpu.VMEM((8, D_MODEL), F32),
                        pltpu.SemaphoreType.DMA((2,)), pltpu.SemaphoreType.DMA(())],
        compiler_params=_cparams(("arbitrary",)),
        name="moe_dispatch",
    )(dest_tiles, pad_dest, h2)


PREP_TK = 512
PREP_CW = 512


def _wprep_kernel(wgu_ref, wd_ref, wg_o, wu_o, wd_o):
    r = lax.broadcasted_iota(jnp.int32, (PREP_CW, PREP_CW // 2), 0)
    c = lax.broadcasted_iota(jnp.int32, (PREP_CW, PREP_CW // 2), 1)
    sel_even = jnp.where(r == 2 * c, 1.0, 0.0).astype(BF16)
    sel_odd = jnp.where(r == 2 * c + 1, 1.0, 0.0).astype(BF16)
    half = PREP_CW // 2
    for j in range(wgu_ref.shape[2] // PREP_CW):
        chunk = wgu_ref[0, :, j * PREP_CW:(j + 1) * PREP_CW].astype(BF16)
        wg_o[0, :, j * half:(j + 1) * half] = _dot(chunk, sel_even).astype(BF16)
        wu_o[0, :, j * half:(j + 1) * half] = _dot(chunk, sel_odd).astype(BF16)
    wd_o[...] = wd_ref[...].astype(BF16)


def _wprep(w_gate_up, w_down):
    n_e, d, two_f = w_gate_up.shape
    f = two_f // 2
    spec = lambda width: pl.BlockSpec((1, PREP_TK, width), lambda e, k: (e, k, 0))
    return pl.pallas_call(
        _wprep_kernel,
        grid=(n_e, d // PREP_TK),
        in_specs=[spec(two_f), spec(w_down.shape[2])],
        out_specs=[spec(f), spec(f), spec(w_down.shape[2])],
        out_shape=[jax.ShapeDtypeStruct((n_e, d, f), BF16), jax.ShapeDtypeStruct((n_e, d, f), BF16),
                   jax.ShapeDtypeStruct(w_down.shape, BF16)],
        compiler_params=_cparams(("arbitrary", "arbitrary")),
        name="moe_weight_prep",
    )(w_gate_up, w_down)


MOE_TF = 512
MOE_NF = D_EXPERT // MOE_TF
MOE_NN = D_MODEL // MOE_TF


def _moe_kernel(be_ref, nv_ref, nu_ref, x_ref, wg_ref, wu_ref, bg_ref, bu_ref, wd_ref, bd_ref, o_ref,
                xb_ref, act_ref):
    blk = pl.program_id(0)
    s = pl.program_id(1)
    n_valid = nv_ref[blk]
    n_sub = x_ref.shape[0] // MOE_SUB

    for t in range(n_sub):
        rows = slice(t * MOE_SUB, (t + 1) * MOE_SUB)
        valid = t * MOE_SUB < n_valid

        @pl.when(jnp.logical_and(valid, s == 0))
        def _():
            xb_ref[rows, :] = x_ref[rows, :].astype(BF16)

        @pl.when(jnp.logical_and(valid, s < MOE_NF))
        def _():
            xs = xb_ref[rows, :]
            glu = jnp.minimum(_dot(xs, wg_ref[0]) + bg_ref[0], SWIGLU_LIMIT)
            lin = jnp.clip(_dot(xs, wu_ref[0]) + bu_ref[0], -SWIGLU_LIMIT, SWIGLU_LIMIT)
            act_ref[s, rows, :] = (glu * _sigmoid(SWIGLU_ALPHA * glu) * (lin + 1.0)).astype(BF16)

        @pl.when(jnp.logical_and(valid, s >= MOE_NF))
        def _():
            act = jnp.concatenate([act_ref[f, rows, :] for f in range(MOE_NF)], axis=1)
            o_ref[rows, :] = _dot(act, wd_ref[0]) + bd_ref[0]

        @pl.when(jnp.logical_and(jnp.logical_not(valid), s >= MOE_NF))
        def _():
            o_ref[rows, :] = jnp.zeros((MOE_SUB, MOE_TF), F32)


def _moe_ffn(x_sorted, block_e, n_valid, n_used, wg, wu, bg3, bu3, wd, bd3):
    rows = MOE_ROWS
    n_blocks = x_sorted.shape[0] // rows
    up = lambda s: jnp.minimum(s, MOE_NF - 1)
    down = lambda s: jnp.maximum(s - MOE_NF, 0)
    grid_spec = pltpu.PrefetchScalarGridSpec(
        num_scalar_prefetch=3,
        grid=(n_blocks, MOE_NF + MOE_NN),
        in_specs=[
            pl.BlockSpec((rows, D_MODEL), lambda b, s, be, nv, nu: (jnp.minimum(b, nu[0] - 1), 0)),
            pl.BlockSpec((1, D_MODEL, MOE_TF), lambda b, s, be, nv, nu: (be[b], 0, up(s))),
            pl.BlockSpec((1, D_MODEL, MOE_TF), lambda b, s, be, nv, nu: (be[b], 0, up(s))),
            pl.BlockSpec((1, 1, MOE_TF), lambda b, s, be, nv, nu: (be[b], 0, up(s))),
            pl.BlockSpec((1, 1, MOE_TF), lambda b, s, be, nv, nu: (be[b], 0, up(s))),
            pl.BlockSpec((1, D_EXPERT, MOE_TF), lambda b, s, be, nv, nu: (be[b], 0, down(s))),
            pl.BlockSpec((1, 1, MOE_TF), lambda b, s, be, nv, nu: (be[b], 0, down(s))),
        ],
        out_specs=pl.BlockSpec((rows, MOE_TF), lambda b, s, be, nv, nu: (b, down(s))),
        scratch_shapes=[pltpu.VMEM((rows, D_MODEL), BF16),
                        pltpu.VMEM((MOE_NF, rows, MOE_TF), BF16)],
    )
    return pl.pallas_call(
        _moe_kernel,
        grid_spec=grid_spec,
        out_shape=jax.ShapeDtypeStruct((n_blocks * rows, D_MODEL), F32),
        compiler_params=_cparams(("arbitrary", "arbitrary")),
        name="moe_ffn",
    )(block_e, n_valid, n_used, x_sorted, wg, wu, bg3, bu3, wd, bd3)


COMBINE_TM = 256


def _combine_kernel(dcur_ref, dnxt_ref, y_hbm, h_ref, gate_ref, g_ref, b_ref, o_ref, yb_ref, sem):
    i = pl.program_id(0)
    n_tiles = pl.num_programs(0)
    tm = h_ref.shape[0]
    n = TOP_K * tm

    def issue_tile(d_ref, slot):
        def issue(j8, c):
            for u in range(ISSUE_UNROLL):
                j = j8 * ISSUE_UNROLL + u
                pltpu.make_async_copy(y_hbm.at[pl.ds(d_ref[0, 0, j], 1), :],
                                      yb_ref.at[slot, pl.ds(j, 1), :], sem.at[slot]).start(priority=u % 2)
            return c
        lax.fori_loop(0, n // ISSUE_UNROLL, issue, 0)

    @pl.when(i == 0)
    def _():
        issue_tile(dcur_ref, 0)

    @pl.when(i + 1 < n_tiles)
    def _():
        issue_tile(dnxt_ref, (i + 1) % 2)

    slot = i % 2
    pltpu.make_async_copy(y_hbm.at[pl.ds(0, n), :], yb_ref.at[slot], sem.at[slot]).wait()
    gate = gate_ref[...]
    acc = DEEPNORM_ALPHA * h_ref[...]
    for k in range(TOP_K):
        acc = acc + gate[:, k:k + 1] * yb_ref[slot, k * tm:(k + 1) * tm, :]
    o_ref[...] = _layer_norm(acc, g_ref[...], b_ref[...])


def _combine(dest3, y_sorted, h2, gates, g, b, tm):
    t = h2.shape[0]
    n_tiles = t // tm
    return pl.pallas_call(
        _combine_kernel,
        grid=(n_tiles,),
        in_specs=[pl.BlockSpec((1, 1, TOP_K * tm), lambda i: (i, 0, 0), memory_space=pltpu.SMEM),
                  pl.BlockSpec((1, 1, TOP_K * tm), lambda i: (jnp.minimum(i + 1, n_tiles - 1), 0, 0),
                               memory_space=pltpu.SMEM),
                  pl.BlockSpec(memory_space=pl.ANY),
                  pl.BlockSpec((tm, D_MODEL), lambda i: (i, 0)),
                  pl.BlockSpec((tm, LANES), lambda i: (i, 0)),
                  _resident((1, D_MODEL)), _resident((1, D_MODEL))],
        out_specs=pl.BlockSpec((tm, D_MODEL), lambda i: (i, 0)),
        out_shape=jax.ShapeDtypeStruct((t, D_MODEL), F32),
        scratch_shapes=[pltpu.VMEM((2, TOP_K * tm, D_MODEL), F32), pltpu.SemaphoreType.DMA((2,))],
        compiler_params=_cparams(("arbitrary",)),
        name="moe_combine_ln3",
    )(dest3, dest3, y_sorted, h2, gates, g, b)


def _dispatch_plan(top_e, n_tok):
    rows = MOE_ROWS
    n_assign = n_tok * TOP_K
    e_flat = top_e.reshape(n_assign)
    onehot = (e_flat[:, None] == jnp.arange(N_EXPERTS, dtype=jnp.int32)[None, :]).astype(jnp.int32)
    csum = jnp.cumsum(onehot, axis=0)
    pos = jnp.sum(csum * onehot, axis=1) - 1
    counts = csum[-1]
    padded = (counts + rows - 1) // rows * rows
    pend = jnp.cumsum(padded)
    pstart = pend - padded
    dest = (pstart[e_flat] + pos).astype(jnp.int32)
    n_blocks = -(-n_assign // rows) + N_EXPERTS
    n_used = (pend[-1] // rows).astype(jnp.int32)
    blk_idx = jnp.arange(n_blocks, dtype=jnp.int32)
    blk_start = blk_idx * rows
    block_e = jnp.minimum(jnp.sum((pend[None, :] <= blk_start[:, None]).astype(jnp.int32), axis=1),
                          N_EXPERTS - 1)
    n_valid = jnp.clip(counts[block_e] - (blk_start - pstart[block_e]), 0, rows)
    n_valid = jnp.where(blk_idx < n_used, n_valid, 0).astype(jnp.int32)
    block_e = jnp.where(blk_idx < n_used, block_e, block_e[n_used - 1]).astype(jnp.int32)
    j = jnp.arange(MOE_SUB, dtype=jnp.int32)[None, :]
    sub_end = (counts + MOE_SUB - 1) // MOE_SUB * MOE_SUB
    pad_dest = jnp.where(counts[:, None] + j < sub_end[:, None], (pstart + counts)[:, None] + j, -1)
    pad_dest = pad_dest.astype(jnp.int32).reshape(1, 1, PAD_SLOTS)
    return dest, pad_dest, block_e, n_valid, n_used.reshape(1), n_blocks * rows


def kernel(x, mem, w_in, b_in, w_gla_a2, b_gla_a, gla_norm_g, hgrn_norm_g, hgrn_lb_logits, w_mix_o,
           w_xq, w_mem_kv, w_xo, w_router, b_router, w_gate_up, b_gate_up, w_down, b_down,
           ln1_g, ln1_b, ln2_g, ln2_b, ln3_g, ln3_b):
    batch, seq, d = x.shape
    t = batch * seq
    x2 = x.reshape(t, d)
    l = 0

    pad = D_IN_PAD - OFF_A1 - GLA_RANK
    w = w_in[l]
    w_gla = w[:, :GA1_SRC].astype(BF16)
    w_hgrn = w[:, GA1_SRC + GLA_RANK:].astype(BF16)
    w_a1 = jnp.pad(w[:, GA1_SRC:GA1_SRC + GLA_RANK].astype(BF16), ((0, 0), (0, PROJ_TN - GLA_RANK)))
    bi = b_in[l]
    b_in_p = jnp.concatenate([bi[:GA1_SRC], bi[GA1_SRC + GLA_RANK:], bi[GA1_SRC:GA1_SRC + GLA_RANK],
                              jnp.zeros((pad,), bi.dtype)])[None, :]
    proj = _in_proj(x2, w_gla, w_hgrn, w_a1, b_in_p, 1024)

    wa2p = jnp.concatenate([w_gla_a2[l], jnp.zeros((LANES - GLA_RANK, GLA_KEY), F32)], axis=0).astype(BF16)
    blk = min(512, seq)
    gla_out = _gla_branch(proj, wa2p, b_gla_a[l][None, :], gla_norm_g[l][None, :], batch, seq, blk)
    merged = _hgrn_branch(proj, gla_out, hgrn_lb_logits, hgrn_norm_g[l][None, :], batch, seq, blk)

    h1 = _out_ln(merged, w_mix_o[l].astype(BF16), x2, ln1_g[l][None, :], ln1_b[l][None, :], 512)

    kv = _matmul_bias(mem.reshape(batch * N_MEM, d), w_mem_kv[l].astype(BF16),
                      jnp.zeros((1, 2 * d), F32), 512, 512, BF16).reshape(batch, N_MEM, 2 * d)
    h2 = _xattn(h1, w_xq[l].astype(BF16), kv, w_xo[l].astype(BF16),
                ln2_g[l][None, :], ln2_b[l][None, :], seq, 512)

    w_rp = jnp.concatenate([w_router[l], jnp.zeros((d, LANES - N_EXPERTS), F32)], axis=1)
    b_rp = jnp.concatenate([b_router[l], jnp.zeros((LANES - N_EXPERTS,), F32)])[None, :]
    idx, gates = _router(h2, w_rp, b_rp, 512)

    dest, pad_dest, block_e, n_valid, n_used, n_rows = _dispatch_plan(idx[:, :TOP_K], t)
    tm_d = min(DISPATCH_TM, t)
    x_sorted = _dispatch(h2, dest.reshape(t // tm_d, 1, TOP_K * tm_d), pad_dest, n_rows)
    wg, wu, wd = _wprep(w_gate_up[l], w_down[l])
    y_sorted = _moe_ffn(x_sorted, block_e, n_valid, n_used, wg, wu,
                        b_gate_up[l][:, None, 0::2], b_gate_up[l][:, None, 1::2],
                        wd, b_down[l][:, None, :])

    tm_c = min(COMBINE_TM, t)
    dest3 = dest.reshape(t // tm_c, tm_c, TOP_K).transpose(0, 2, 1).reshape(t // tm_c, 1, TOP_K * tm_c)
    out = _combine(dest3, y_sorted, h2, gates, ln3_g[l][None, :], ln3_b[l][None, :], tm_c)
    return out.reshape(batch, seq, d)
```

```python
import jax
import jax.numpy as jnp
from jax import lax
from jax.experimental import pallas as pl
from jax.experimental.pallas import tpu as pltpu

F32 = jnp.float32
BF16 = jnp.bfloat16

D_MODEL = 2048
N_MEM = 256
GLA_HEADS = 4
GLA_KEY = D_MODEL // 2
GLA_VAL = D_MODEL
GLA_DK = GLA_KEY // GLA_HEADS
GLA_DV = GLA_VAL // GLA_HEADS
GLA_RANK = 16
GLA_TAU = 16.0
HGRN_DK = 128
HGRN_HEADS = D_MODEL // HGRN_DK
HGRN_DV = D_MODEL // HGRN_HEADS
X_HEADS = 4
X_DH = D_MODEL // X_HEADS
N_EXPERTS = 32
TOP_K = 4
D_EXPERT = D_MODEL
SWIGLU_LIMIT = 7.0
SWIGLU_ALPHA = 1.702
DEPTH = 1
DEEPNORM_ALPHA = (2.0 * DEPTH) ** 0.25
NORM_EPS = 1e-5

LANES = 128
CHUNK = 64
SUB = 16
N_SUB = CHUNK // SUB
SUB_SHIFT = SUB.bit_length() - 1

OFF_GQ = 0
OFF_GK = GLA_KEY
OFF_GV = 2 * GLA_KEY
OFF_GR = OFF_GV + GLA_VAL
OFF_HQ = OFF_GR + GLA_VAL
OFF_HF = OFF_HQ + D_MODEL
OFF_HI = OFF_HF + D_MODEL
OFF_HG = OFF_HI + D_MODEL
OFF_MA = OFF_HG + D_MODEL
OFF_MB = OFF_MA + D_MODEL
OFF_A1 = OFF_MB + D_MODEL
PROJ_TN = 512
D_IN_PAD = -(-(OFF_A1 + LANES) // PROJ_TN) * PROJ_TN
GA1_SRC = OFF_GR + GLA_VAL

VMEM_LIMIT = 56 * 1024 * 1024


def _cparams(sem):
    return pltpu.CompilerParams(dimension_semantics=sem, vmem_limit_bytes=VMEM_LIMIT)


def _dot(a, b):
    return jnp.dot(a, b, preferred_element_type=F32)


def _dot_nt(a, b):
    return lax.dot_general(a, b, (((1,), (1,)), ((), ())), preferred_element_type=F32)


def _dot_tn(a, b):
    return lax.dot_general(a, b, (((0,), (0,)), ((), ())), preferred_element_type=F32)


def _sigmoid(x):
    return 0.5 * jnp.tanh(0.5 * x) + 0.5


def _log_sigmoid(x):
    return jnp.minimum(x, 0.0) - jnp.log1p(jnp.exp(-jnp.abs(x)))


def _layer_norm(r, g, b):
    mu = jnp.mean(r, axis=-1, keepdims=True)
    c = r - mu
    var = jnp.mean(c * c, axis=-1, keepdims=True)
    return c * lax.rsqrt(var + NORM_EPS) * g + b


def _split_bf16(x):
    hi = x.astype(BF16)
    lo = (x - hi.astype(F32)).astype(BF16)
    return hi, lo


def _proj_kernel(x_ref, w_ref, b_ref, o_ref, xb_ref):
    @pl.when(pl.program_id(1) == 0)
    def _():
        xb_ref[...] = x_ref[...].astype(BF16)

    o_ref[...] = (_dot(xb_ref[...], w_ref[...]) + b_ref[...]).astype(o_ref.dtype)


def _matmul_bias(x, w, b, tm, tn, out_dtype):
    m, k = x.shape
    n = w.shape[1]
    tm = min(tm, m)
    return pl.pallas_call(
        _proj_kernel,
        grid=(m // tm, n // tn),
        in_specs=[pl.BlockSpec((tm, k), lambda i, j: (i, 0)),
                  pl.BlockSpec((k, tn), lambda i, j: (0, j)),
                  pl.BlockSpec((1, tn), lambda i, j: (0, j))],
        out_specs=pl.BlockSpec((tm, tn), lambda i, j: (i, j)),
        out_shape=jax.ShapeDtypeStruct((m, n), out_dtype),
        scratch_shapes=[pltpu.VMEM((tm, k), BF16)],
        compiler_params=_cparams(("arbitrary", "arbitrary")),
        name="matmul_bias",
    )(x, w, b)


PROJ_NG = OFF_HQ // PROJ_TN
PROJ_NH = (OFF_A1 - OFF_HQ) // PROJ_TN


def _in_proj_kernel(x_ref, wg_ref, wh_ref, wa_ref, b_ref, o_ref, xb_ref):
    j = pl.program_id(1)

    @pl.when(j == 0)
    def _():
        xb_ref[...] = x_ref[...].astype(BF16)

    def tile(w_ref):
        o_ref[...] = (_dot(xb_ref[...], w_ref[...]) + b_ref[...]).astype(o_ref.dtype)

    @pl.when(j < PROJ_NG)
    def _():
        tile(wg_ref)

    @pl.when(jnp.logical_and(j >= PROJ_NG, j < PROJ_NG + PROJ_NH))
    def _():
        tile(wh_ref)

    @pl.when(j >= PROJ_NG + PROJ_NH)
    def _():
        tile(wa_ref)


def _in_proj(x, wg, wh, wa, b, tm):
    m, k = x.shape
    tm = min(tm, m)
    tn = PROJ_TN
    return pl.pallas_call(
        _in_proj_kernel,
        grid=(m // tm, D_IN_PAD // tn),
        in_specs=[pl.BlockSpec((tm, k), lambda i, j: (i, 0)),
                  pl.BlockSpec((k, tn), lambda i, j: (0, jnp.minimum(j, PROJ_NG - 1))),
                  pl.BlockSpec((k, tn), lambda i, j: (0, jnp.clip(j - PROJ_NG, 0, PROJ_NH - 1))),
                  pl.BlockSpec((k, tn), lambda i, j: (0, 0)),
                  pl.BlockSpec((1, tn), lambda i, j: (0, j))],
        out_specs=pl.BlockSpec((tm, tn), lambda i, j: (i, j)),
        out_shape=jax.ShapeDtypeStruct((m, D_IN_PAD), BF16),
        scratch_shapes=[pltpu.VMEM((tm, k), BF16)],
        compiler_params=_cparams(("arbitrary", "arbitrary")),
        name="in_proj",
    )(x, wg, wh, wa, b)


CUM_ROWS = CHUNK + 16


def _cum_matrix():
    r = lax.broadcasted_iota(jnp.int32, (CUM_ROWS, CHUNK), 0)
    c = lax.broadcasted_iota(jnp.int32, (CUM_ROWS, CHUNK), 1)
    within = jnp.logical_and(c <= r, (c >> SUB_SHIFT) == (r >> SUB_SHIFT))
    bound = c < (r - CHUNK) * SUB
    one = jnp.logical_or(jnp.logical_and(r < CHUNK, within), jnp.logical_and(r >= CHUNK, bound))
    return jnp.where(one, 1.0, 0.0).astype(BF16)


def _scan_block(q, k, v, lg, states, n_heads):
    rows, dkt = q.shape
    dk = dkt // n_heads
    dv = v.shape[1] // n_heads
    n_chunks = rows // CHUNK
    cmat = _cum_matrix()
    hi, lo = _split_bf16(lg)
    us, bounds = [], []
    for c in range(n_chunks):
        c2 = _dot(cmat, hi[c * CHUNK:(c + 1) * CHUNK]) + _dot(cmat, lo[c * CHUNK:(c + 1) * CHUNK])
        us.append(c2[:CHUNK])
        bounds.append(c2[CHUNK:CHUNK + 8])
    u = jnp.concatenate(us, axis=0)
    c_sub = [b[:N_SUB] for b in bounds]
    c_all = [b[N_SUB:N_SUB + 1] for b in bounds]
    sub_id = lax.broadcasted_iota(jnp.int32, (N_SUB, dkt), 0)

    def spread(tables):
        return jnp.concatenate([jnp.broadcast_to(t[j:j + 1, :], (SUB, dkt))
                                for t in tables for j in range(N_SUB)], axis=0)

    qe = q * jnp.exp(u)
    kq = k * jnp.exp(-u)
    qd = (qe * spread([jnp.exp(t) for t in c_sub])).astype(BF16)
    kd = (kq * spread([jnp.exp(a - t) for a, t in zip(c_all, c_sub)])).astype(BF16)
    row_in = lax.broadcasted_iota(jnp.int32, (rows, dkt), 0) & (CHUNK - 1)
    q_sub, r_sub = [], []
    for i in range(N_SUB):
        fac = [jnp.where(sub_id <= i, jnp.exp(t[i:i + 1, :] - t), 0.0) for t in c_sub]
        r_sub.append((kq * spread(fac)).astype(BF16))
        q_sub.append(jnp.where((row_in >> SUB_SHIFT) == i, qe, 0.0).astype(BF16))
    causal = (lax.broadcasted_iota(jnp.int32, (CHUNK, CHUNK), 1)
              <= lax.broadcasted_iota(jnp.int32, (CHUNK, CHUNK), 0))
    pairs = [(c, h) for h in range(n_heads) for c in range(n_chunks)]
    rs = lambda c: slice(c * CHUNK, (c + 1) * CHUNK)
    ks = lambda h: slice(h * dk, (h + 1) * dk)
    vs = lambda h: slice(h * dv, (h + 1) * dv)
    q_stack = [jnp.concatenate([x[:, ks(h)] for x in q_sub], axis=1) for h in range(n_heads)]
    r_stack = [jnp.concatenate([x[:, ks(h)] for x in r_sub], axis=1) for h in range(n_heads)]
    upd = {p: _dot_tn(v[rs(p[0]), vs(p[1])], kd[rs(p[0]), ks(p[1])]) for p in pairs}
    a = {p: _dot_nt(q_stack[p[1]][rs(p[0])], r_stack[p[1]][rs(p[0])]) for p in pairs}
    a = {p: jnp.where(causal, a[p], 0.0).astype(BF16) for p in pairs}
    s_in = {}
    new_states = []
    for h in range(n_heads):
        s = states[h]
        for c in range(n_chunks):
            s_in[(c, h)] = s.astype(BF16)
            s = s * jnp.exp(c_all[c][:, ks(h)]) + upd[(c, h)]
        new_states.append(s)
    inter = {p: _dot_nt(qd[rs(p[0]), ks(p[1])], s_in[p]) for p in pairs}
    intra = {p: _dot(a[p], v[rs(p[0]), vs(p[1])]) for p in pairs}
    o = jnp.concatenate([jnp.concatenate([inter[(c, h)] + intra[(c, h)] for h in range(n_heads)], axis=1)
                         for c in range(n_chunks)], axis=0)
    return o, new_states


def _group_rmsnorm(o, g, n_heads):
    dv = o.shape[1] // n_heads
    parts = []
    for h in range(n_heads):
        oh = o[:, h * dv:(h + 1) * dv]
        parts.append(oh * lax.rsqrt(jnp.mean(oh * oh, axis=-1, keepdims=True) + NORM_EPS) * g)
    return jnp.concatenate(parts, axis=1)


SPLIT_CW = 256
PREP_MAX_ROWS = 512


def _split_gate_up(wgu_ref, wg_o, wu_o):
    n_rows, two_f = wgu_ref.shape
    half = SPLIT_CW // 2
    r = lax.broadcasted_iota(jnp.int32, (SPLIT_CW, SPLIT_CW), 0)
    c = lax.broadcasted_iota(jnp.int32, (SPLIT_CW, SPLIT_CW), 1)
    pick = jnp.logical_or(jnp.logical_and(c < half, r == 2 * c),
                          jnp.logical_and(c >= half, r == 2 * (c - half) + 1))
    sel = jnp.where(pick, 1.0, 0.0).astype(BF16)
    n_blocks = two_f // SPLIT_CW
    x = jnp.concatenate([wgu_ref[:, j * SPLIT_CW:(j + 1) * SPLIT_CW].astype(BF16) for j in range(n_blocks)],
                        axis=0)
    y = _dot(x, sel).astype(BF16)
    for j in range(n_blocks):
        wg_o[:, j * half:(j + 1) * half] = y[j * n_rows:(j + 1) * n_rows, :half]
        wu_o[:, j * half:(j + 1) * half] = y[j * n_rows:(j + 1) * n_rows, half:]


def _side_rows(total_rows, n_steps):
    rows = total_rows // n_steps
    ok = rows * n_steps == total_rows and 16 <= rows <= PREP_MAX_ROWS and rows % 16 == 0
    return rows if ok else None


def _gla_kernel(q_ref, k_ref, v_ref, r_ref, ma_ref, a1_ref, wa2_ref, ba_ref, g_ref, *rest):
    if len(rest) == 4:
        wd_ref, o_ref, wd_o, s_ref = rest
    else:
        o_ref, s_ref = rest

    @pl.when(pl.program_id(2) == 0)
    def _():
        s_ref[...] = jnp.zeros_like(s_ref)

    z = _dot(a1_ref[...], wa2_ref[...]) + ba_ref[...]
    lg = _log_sigmoid(z) * (1.0 / GLA_TAU)
    q = q_ref[...].astype(F32) * (GLA_DK ** -0.5)
    k = k_ref[...].astype(F32)
    o, new_states = _scan_block(q, k, v_ref[...], lg, [s_ref[...]], 1)
    s_ref[...] = new_states[0]
    o = _group_rmsnorm(o, g_ref[...], 1)
    r = r_ref[...].astype(F32)
    gate = _sigmoid(ma_ref[...].astype(F32))
    o_ref[...] = (gate * (o * (r * _sigmoid(r)))).astype(o_ref.dtype)
    if len(rest) == 4:
        wd_o[...] = wd_ref[...].astype(BF16)


def _gla_branch(proj, wa2p, ba, gla_g, batch, seq, blk, w_side=None):
    nb = seq // blk
    row = lambda b, h, c: b * nb + c
    step = lambda b, h, c: (b * GLA_HEADS + h) * nb + c
    kb, vb = GLA_DK, GLA_DV
    in_specs = [
        pl.BlockSpec((blk, kb), lambda b, h, c: (row(b, h, c), OFF_GQ // kb + h)),
        pl.BlockSpec((blk, kb), lambda b, h, c: (row(b, h, c), OFF_GK // kb + h)),
        pl.BlockSpec((blk, vb), lambda b, h, c: (row(b, h, c), OFF_GV // vb + h)),
        pl.BlockSpec((blk, vb), lambda b, h, c: (row(b, h, c), OFF_GR // vb + h)),
        pl.BlockSpec((blk, vb), lambda b, h, c: (row(b, h, c), OFF_MA // vb + h)),
        pl.BlockSpec((blk, LANES), lambda b, h, c: (row(b, h, c), OFF_A1 // LANES)),
        pl.BlockSpec((LANES, kb), lambda b, h, c: (0, h)),
        pl.BlockSpec((1, kb), lambda b, h, c: (0, h)),
        pl.BlockSpec((1, vb), lambda b, h, c: (0, 0)),
    ]
    out_specs = [pl.BlockSpec((blk, vb), lambda b, h, c: (row(b, h, c), h))]
    out_shape = [jax.ShapeDtypeStruct((batch * seq, GLA_VAL), BF16)]
    operands = [proj, proj, proj, proj, proj, proj, wa2p, ba, gla_g]
    if w_side is not None:
        side = _side_rows(w_side.shape[0], batch * GLA_HEADS * nb)
        spec = pl.BlockSpec((side, w_side.shape[1]), lambda b, h, c: (step(b, h, c), 0))
        in_specs.append(spec)
        out_specs.append(spec)
        out_shape.append(jax.ShapeDtypeStruct(w_side.shape, BF16))
        operands.append(w_side)
    return pl.pallas_call(
        _gla_kernel,
        grid=(batch, GLA_HEADS, nb),
        in_specs=in_specs,
        out_specs=out_specs,
        out_shape=out_shape,
        scratch_shapes=[pltpu.VMEM((vb, kb), F32)],
        compiler_params=_cparams(("arbitrary", "arbitrary", "arbitrary")),
        name="gla_scan",
    )(*operands)


HGRN_HP = 2


def _hgrn_kernel(q_ref, f_ref, i_ref, g_ref, mb_ref, gla_ref, lbl_ref, ng_ref, *rest):
    if len(rest) == 5:
        wgu_ref, o_ref, wg_o, wu_o, s_ref = rest
    else:
        o_ref, s_ref = rest

    @pl.when(pl.program_id(2) == 0)
    def _():
        s_ref[...] = jnp.zeros_like(s_ref)

    lbl = lbl_ref[...].astype(F32)
    e = jnp.exp(lbl - jnp.max(lbl, axis=0, keepdims=True))
    lb = e[0:1, :] / jnp.sum(e, axis=0, keepdims=True)

    hq = q_ref[...].astype(F32)
    hf = f_ref[...].astype(F32)
    q = hq * _sigmoid(hq)
    th = jnp.tanh(0.5 * hf)
    half = 0.5 * (1.0 - lb)
    lg = jnp.log(lb + half * (1.0 + th))
    k = half * (1.0 - th)
    o, new_states = _scan_block(q, k, i_ref[...], lg, [s_ref[h] for h in range(HGRN_HP)], HGRN_HP)
    for h in range(HGRN_HP):
        s_ref[h] = new_states[h]
    o = _group_rmsnorm(o, ng_ref[...], HGRN_HP)
    o = o * _sigmoid(g_ref[...].astype(F32))
    merged = gla_ref[...].astype(F32) + _sigmoid(mb_ref[...].astype(F32)) * o
    o_ref[...] = merged.astype(o_ref.dtype)
    if len(rest) == 5:
        _split_gate_up(wgu_ref, wg_o, wu_o)


def _hgrn_branch(proj, gla_out, lb_logits, hgrn_g, batch, seq, blk, w_side=None):
    nb = seq // blk
    w = HGRN_HP * HGRN_DK
    n_groups = HGRN_HEADS // HGRN_HP
    row = lambda b, h, c: b * nb + c
    step = lambda b, h, c: (b * n_groups + h) * nb + c
    in_specs = [
        pl.BlockSpec((blk, w), lambda b, h, c: (row(b, h, c), OFF_HQ // w + h)),
        pl.BlockSpec((blk, w), lambda b, h, c: (row(b, h, c), OFF_HF // w + h)),
        pl.BlockSpec((blk, w), lambda b, h, c: (row(b, h, c), OFF_HI // w + h)),
        pl.BlockSpec((blk, w), lambda b, h, c: (row(b, h, c), OFF_HG // w + h)),
        pl.BlockSpec((blk, w), lambda b, h, c: (row(b, h, c), OFF_MB // w + h)),
        pl.BlockSpec((blk, w), lambda b, h, c: (row(b, h, c), h)),
        pl.BlockSpec((DEPTH + 1, w), lambda b, h, c: (0, h)),
        pl.BlockSpec((1, HGRN_DV), lambda b, h, c: (0, 0)),
    ]
    out_specs = [pl.BlockSpec((blk, w), lambda b, h, c: (row(b, h, c), h))]
    out_shape = [jax.ShapeDtypeStruct((batch * seq, D_MODEL), BF16)]
    operands = [proj, proj, proj, proj, proj, gla_out, lb_logits, hgrn_g]
    if w_side is not None:
        total, two_f = w_side.shape
        side = _side_rows(total, batch * n_groups * nb)
        in_specs.append(pl.BlockSpec((side, two_f), lambda b, h, c: (step(b, h, c), 0)))
        for _ in range(2):
            out_specs.append(pl.BlockSpec((side, two_f // 2), lambda b, h, c: (step(b, h, c), 0)))
            out_shape.append(jax.ShapeDtypeStruct((total, two_f // 2), BF16))
        operands.append(w_side)
    return pl.pallas_call(
        _hgrn_kernel,
        grid=(batch, n_groups, nb),
        in_specs=in_specs,
        out_specs=out_specs,
        out_shape=out_shape,
        scratch_shapes=[pltpu.VMEM((HGRN_HP, HGRN_DV, HGRN_DK), F32)],
        compiler_params=_cparams(("arbitrary", "arbitrary", "arbitrary")),
        name="hgrn_scan",
    )(*operands)


def _out_ln_kernel(m_ref, w_ref, x_ref, g_ref, b_ref, o_ref):
    y = _dot(m_ref[...], w_ref[...])
    o_ref[...] = _layer_norm(DEEPNORM_ALPHA * x_ref[...] + y, g_ref[...], b_ref[...])


def _resident(shape):
    return pl.BlockSpec(shape, lambda *_: (0,) * len(shape), pipeline_mode=pl.Buffered(1))


def _out_ln(merged, w_o, x2, g, b, tm):
    t = merged.shape[0]
    tm = min(tm, t)
    return pl.pallas_call(
        _out_ln_kernel,
        grid=(t // tm,),
        in_specs=[pl.BlockSpec((tm, D_MODEL), lambda i: (i, 0)),
                  _resident((D_MODEL, D_MODEL)),
                  pl.BlockSpec((tm, D_MODEL), lambda i: (i, 0)),
                  _resident((1, D_MODEL)), _resident((1, D_MODEL))],
        out_specs=pl.BlockSpec((tm, D_MODEL), lambda i: (i, 0)),
        out_shape=jax.ShapeDtypeStruct((t, D_MODEL), F32),
        compiler_params=_cparams(("arbitrary",)),
        name="wo_ln1",
    )(merged, w_o, x2, g, b)


def _xattn_kernel(h_ref, wq_ref, kv_ref, wo_ref, g_ref, b_ref, o_ref):
    h = h_ref[...]
    q = _dot(h.astype(BF16), wq_ref[...]).astype(BF16)
    outs = []
    for hh in range(X_HEADS):
        sl = slice(hh * X_DH, (hh + 1) * X_DH)
        kh = kv_ref[0, :, sl]
        vh = kv_ref[0, :, D_MODEL + hh * X_DH:D_MODEL + (hh + 1) * X_DH]
        s = _dot_nt(q[:, sl], kh) * (X_DH ** -0.5)
        p = jnp.exp(s - jnp.max(s, axis=-1, keepdims=True))
        l = jnp.sum(p, axis=-1, keepdims=True)
        outs.append((_dot(p.astype(BF16), vh) * (1.0 / l)).astype(BF16))
    o = jnp.concatenate(outs, axis=1)
    y = _dot(o, wo_ref[...])
    o_ref[...] = _layer_norm(DEEPNORM_ALPHA * h + y, g_ref[...], b_ref[...])


def _xattn(h1, w_xq, kv, w_xo, g, b, seq, tm):
    t = h1.shape[0]
    tm = min(tm, seq)
    per_b = seq // tm
    return pl.pallas_call(
        _xattn_kernel,
        grid=(t // tm,),
        in_specs=[pl.BlockSpec((tm, D_MODEL), lambda i: (i, 0)),
                  _resident((D_MODEL, D_MODEL)),
                  pl.BlockSpec((1, N_MEM, 2 * D_MODEL), lambda i: (i // per_b, 0, 0)),
                  _resident((D_MODEL, D_MODEL)),
                  _resident((1, D_MODEL)), _resident((1, D_MODEL))],
        out_specs=pl.BlockSpec((tm, D_MODEL), lambda i: (i, 0)),
        out_shape=jax.ShapeDtypeStruct((t, D_MODEL), F32),
        compiler_params=_cparams(("arbitrary",)),
        name="xattn_ln2",
    )(h1, w_xq, kv, w_xo, g, b)


def _router_kernel(h_ref, w_ref, b_ref, idx_ref, gate_ref):
    hh, hl = _split_bf16(h_ref[...])
    wh, wl = _split_bf16(w_ref[...])
    logits = _dot(hh, wh) + _dot(hh, wl) + _dot(hl, wh) + b_ref[...]
    lane = lax.broadcasted_iota(jnp.int32, logits.shape, 1)
    neg = jnp.float32(-jnp.inf)
    logits = jnp.where(lane < N_EXPERTS, logits, neg)
    idx_out = jnp.zeros(logits.shape, jnp.int32)
    tops = []
    for k in range(TOP_K):
        m = jnp.max(logits, axis=-1, keepdims=True)
        idx = jnp.min(jnp.where(logits == m, lane, LANES), axis=-1, keepdims=True)
        idx_out = jnp.where(lane == k, idx, idx_out)
        logits = jnp.where(lane == idx, neg, logits)
        tops.append(m)
    ex = [jnp.exp(m - tops[0]) for m in tops]
    den = ex[0] + ex[1] + ex[2] + ex[3]
    gate_out = jnp.zeros(logits.shape, F32)
    for k in range(TOP_K):
        gate_out = jnp.where(lane == k, ex[k] / den, gate_out)
    idx_ref[...] = idx_out
    gate_ref[...] = gate_out


def _router(h2, w_rp, b_rp, tm):
    t = h2.shape[0]
    tm = min(tm, t)
    return pl.pallas_call(
        _router_kernel,
        grid=(t // tm,),
        in_specs=[pl.BlockSpec((tm, D_MODEL), lambda i: (i, 0)),
                  _resident((D_MODEL, LANES)), _resident((1, LANES))],
        out_specs=[pl.BlockSpec((tm, LANES), lambda i: (i, 0)),
                   pl.BlockSpec((tm, LANES), lambda i: (i, 0))],
        out_shape=[jax.ShapeDtypeStruct((t, LANES), jnp.int32),
                   jax.ShapeDtypeStruct((t, LANES), F32)],
        compiler_params=_cparams(("arbitrary",)),
        name="router_top4",
    )(h2, w_rp, b_rp)


DISPATCH_TM = 256
ISSUE_UNROLL = 8
MOE_ROWS = 1024
MOE_SUB = 256
PAD_SLOTS = N_EXPERTS * MOE_SUB


def _dispatch_kernel(dest_ref, pad_ref, h_ref, xs_hbm, stage_ref, zero_ref, sem, zsem):
    i = pl.program_id(0)
    n_tiles = pl.num_programs(0) - 1
    tm = h_ref.shape[0]
    n = TOP_K * tm

    @pl.when(i < n_tiles)
    def _():
        slot = i % 2
        stage_ref[slot] = h_ref[...]

        def issue(j8, c):
            for u in range(ISSUE_UNROLL):
                j = j8 * ISSUE_UNROLL + u
                tok = j8 * (ISSUE_UNROLL // TOP_K) + u // TOP_K
                pltpu.make_async_copy(stage_ref.at[slot, pl.ds(tok, 1), :],
                                      xs_hbm.at[pl.ds(dest_ref[0, 0, j], 1), :],
                                      sem.at[slot]).start(priority=u % 2)
            return c
        lax.fori_loop(0, n // ISSUE_UNROLL, issue, 0)

    @pl.when(i >= 1)
    def _():
        for _ in range(TOP_K):
            pltpu.make_async_copy(stage_ref.at[0, pl.ds(0, tm), :], xs_hbm.at[pl.ds(0, tm), :],
                                  sem.at[(i + 1) % 2]).wait()

    @pl.when(i == n_tiles)
    def _():
        zero_ref[...] = jnp.zeros_like(zero_ref)

        def pad_copy(j):
            return pltpu.make_async_copy(zero_ref.at[pl.ds(0, 1), :],
                                         xs_hbm.at[pl.ds(pad_ref[0, 0, j], 1), :], zsem)

        def issue(j, c):
            @pl.when(pad_ref[0, 0, j] >= 0)
            def _():
                pad_copy(j).start()
            return c
        lax.fori_loop(0, PAD_SLOTS, issue, 0)

        def drain(j, c):
            @pl.when(pad_ref[0, 0, j] >= 0)
            def _():
                pad_copy(j).wait()
            return c
        lax.fori_loop(0, PAD_SLOTS, drain, 0)


def _dispatch(h2, dest_tiles, pad_dest, n_rows):
    t = h2.shape[0]
    tm = min(DISPATCH_TM, t)
    n_tiles = t // tm
    last = n_tiles - 1
    return pl.pallas_call(
        _dispatch_kernel,
        grid=(n_tiles + 1,),
        in_specs=[pl.BlockSpec((1, 1, TOP_K * tm), lambda i: (jnp.minimum(i, last), 0, 0),
                               memory_space=pltpu.SMEM),
                  pl.BlockSpec((1, 1, PAD_SLOTS), lambda i: (0, 0, 0), memory_space=pltpu.SMEM),
                  pl.BlockSpec((tm, D_MODEL), lambda i: (jnp.minimum(i, last), 0))],
        out_specs=pl.BlockSpec(memory_space=pl.ANY),
        out_shape=jax.ShapeDtypeStruct((n_rows, D_MODEL), F32),
        scratch_shapes=[pltpu.VMEM((2, tm, D_MODEL), F32), pltpu.VMEM((8, D_MODEL), F32),
                        pltpu.SemaphoreType.DMA((2,)), pltpu.SemaphoreType.DMA(())],
        compiler_params=_cparams(("arbitrary",)),
        name="moe_dispatch",
    )(dest_tiles, pad_dest, h2)


PREP_TK = 256


def _wprep_kernel(wgu_ref, wd_ref, wg_o, wu_o, wd_o):
    _split_gate_up(wgu_ref, wg_o, wu_o)
    wd_o[...] = wd_ref[...].astype(BF16)


def _wprep(wgu2, wd2):
    total, two_f = wgu2.shape
    spec = lambda width: pl.BlockSpec((PREP_TK, width), lambda i: (i, 0))
    assert wd2.shape[0] == total, "gate/up and down weights are walked with one row index"
    return pl.pallas_call(
        _wprep_kernel,
        grid=(total // PREP_TK,),
        in_specs=[spec(two_f), spec(wd2.shape[1])],
        out_specs=[spec(two_f // 2), spec(two_f // 2), spec(wd2.shape[1])],
        out_shape=[jax.ShapeDtypeStruct((total, two_f // 2), BF16),
                   jax.ShapeDtypeStruct((total, two_f // 2), BF16),
                   jax.ShapeDtypeStruct(wd2.shape, BF16)],
        compiler_params=_cparams(("arbitrary",)),
        name="moe_weight_prep",
    )(wgu2, wd2)


MOE_TF = 512
MOE_NF = D_EXPERT // MOE_TF
MOE_NN = D_MODEL // MOE_TF


def _moe_kernel(be_ref, nv_ref, nu_ref, x_ref, wg_ref, wu_ref, bg_ref, bu_ref, wd_ref, bd_ref, o_ref,
                xb_ref, act_ref):
    blk = pl.program_id(0)
    s = pl.program_id(1)
    n_valid = nv_ref[blk]
    n_rows = x_ref.shape[0]
    n_sub = n_rows // MOE_SUB
    full = n_valid == n_rows

    def cast_x(rows):
        xb_ref[rows, :] = x_ref[rows, :].astype(BF16)

    def gate_up(rows):
        xs = xb_ref[rows, :]
        glu = jnp.minimum(_dot(xs, wg_ref[0]) + bg_ref[0], SWIGLU_LIMIT)
        lin = jnp.clip(_dot(xs, wu_ref[0]) + bu_ref[0], -SWIGLU_LIMIT, SWIGLU_LIMIT)
        act_ref[s, rows, :] = (glu * _sigmoid(SWIGLU_ALPHA * glu) * (lin + 1.0)).astype(BF16)

    def down(rows):
        act = jnp.concatenate([act_ref[f, rows, :] for f in range(MOE_NF)], axis=1)
        o_ref[rows, :] = _dot(act, wd_ref[0]) + bd_ref[0]

    every = slice(0, n_rows)
    pl.when(jnp.logical_and(full, s == 0))(lambda: cast_x(every))
    pl.when(jnp.logical_and(full, s < MOE_NF))(lambda: gate_up(every))
    pl.when(jnp.logical_and(full, s >= MOE_NF))(lambda: down(every))
    for t in range(n_sub):
        rows = slice(t * MOE_SUB, (t + 1) * MOE_SUB)
        valid = jnp.logical_and(t * MOE_SUB < n_valid, jnp.logical_not(full))
        pl.when(jnp.logical_and(valid, s == 0))(lambda rows=rows: cast_x(rows))
        pl.when(jnp.logical_and(valid, s < MOE_NF))(lambda rows=rows: gate_up(rows))
        pl.when(jnp.logical_and(valid, s >= MOE_NF))(lambda rows=rows: down(rows))

        @pl.when(jnp.logical_and(t * MOE_SUB >= n_valid, s >= MOE_NF))
        def _(rows=rows):
            o_ref[rows, :] = jnp.zeros((MOE_SUB, MOE_TF), F32)


def _moe_ffn(x_sorted, block_e, n_valid, n_used, wg, wu, bg3, bu3, wd, bd3):
    rows = MOE_ROWS
    n_blocks = x_sorted.shape[0] // rows
    up = lambda s: jnp.minimum(s, MOE_NF - 1)
    down = lambda s: jnp.maximum(s - MOE_NF, 0)
    grid_spec = pltpu.PrefetchScalarGridSpec(
        num_scalar_prefetch=3,
        grid=(n_blocks, MOE_NF + MOE_NN),
        in_specs=[
            pl.BlockSpec((rows, D_MODEL), lambda b, s, be, nv, nu: (jnp.minimum(b, nu[0] - 1), 0)),
            pl.BlockSpec((1, D_MODEL, MOE_TF), lambda b, s, be, nv, nu: (be[b], 0, up(s))),
            pl.BlockSpec((1, D_MODEL, MOE_TF), lambda b, s, be, nv, nu: (be[b], 0, up(s))),
            pl.BlockSpec((1, 1, MOE_TF), lambda b, s, be, nv, nu: (be[b], 0, up(s))),
            pl.BlockSpec((1, 1, MOE_TF), lambda b, s, be, nv, nu: (be[b], 0, up(s))),
            pl.BlockSpec((1, D_EXPERT, MOE_TF), lambda b, s, be, nv, nu: (be[b], 0, down(s))),
            pl.BlockSpec((1, 1, MOE_TF), lambda b, s, be, nv, nu: (be[b], 0, down(s))),
        ],
        out_specs=pl.BlockSpec((rows, MOE_TF), lambda b, s, be, nv, nu: (b, down(s))),
        scratch_shapes=[pltpu.VMEM((rows, D_MODEL), BF16),
                        pltpu.VMEM((MOE_NF, rows, MOE_TF), BF16)],
    )
    return pl.pallas_call(
        _moe_kernel,
        grid_spec=grid_spec,
        out_shape=jax.ShapeDtypeStruct((n_blocks * rows, D_MODEL), F32),
        compiler_params=_cparams(("arbitrary", "arbitrary")),
        name="moe_ffn",
    )(block_e, n_valid, n_used, x_sorted, wg, wu, bg3, bu3, wd, bd3)


COMBINE_TM = 256


def _combine_kernel(dcur_ref, dnxt_ref, y_hbm, h_ref, gate_ref, g_ref, b_ref, o_ref, yb_ref, sem):
    i = pl.program_id(0)
    n_tiles = pl.num_programs(0)
    tm = h_ref.shape[0]
    n = TOP_K * tm

    def issue_tile(d_ref, slot):
        def issue(j8, c):
            for u in range(ISSUE_UNROLL):
                j = j8 * ISSUE_UNROLL + u
                pltpu.make_async_copy(y_hbm.at[pl.ds(d_ref[0, 0, j], 1), :],
                                      yb_ref.at[slot, pl.ds(j, 1), :], sem.at[slot]).start(priority=u % 2)
            return c
        lax.fori_loop(0, n // ISSUE_UNROLL, issue, 0)

    @pl.when(i == 0)
    def _():
        issue_tile(dcur_ref, 0)

    @pl.when(i + 1 < n_tiles)
    def _():
        issue_tile(dnxt_ref, (i + 1) % 2)

    slot = i % 2
    pltpu.make_async_copy(y_hbm.at[pl.ds(0, n), :], yb_ref.at[slot], sem.at[slot]).wait()
    gate = gate_ref[...]
    acc = DEEPNORM_ALPHA * h_ref[...]
    for k in range(TOP_K):
        acc = acc + gate[:, k:k + 1] * yb_ref[slot, k * tm:(k + 1) * tm, :]
    o_ref[...] = _layer_norm(acc, g_ref[...], b_ref[...])


def _combine(dest3, y_sorted, h2, gates, g, b, tm):
    t = h2.shape[0]
    n_tiles = t // tm
    return pl.pallas_call(
        _combine_kernel,
        grid=(n_tiles,),
        in_specs=[pl.BlockSpec((1, 1, TOP_K * tm), lambda i: (i, 0, 0), memory_space=pltpu.SMEM),
                  pl.BlockSpec((1, 1, TOP_K * tm), lambda i: (jnp.minimum(i + 1, n_tiles - 1), 0, 0),
                               memory_space=pltpu.SMEM),
                  pl.BlockSpec(memory_space=pl.ANY),
                  pl.BlockSpec((tm, D_MODEL), lambda i: (i, 0)),
                  pl.BlockSpec((tm, LANES), lambda i: (i, 0)),
                  _resident((1, D_MODEL)), _resident((1, D_MODEL))],
        out_specs=pl.BlockSpec((tm, D_MODEL), lambda i: (i, 0)),
        out_shape=jax.ShapeDtypeStruct((t, D_MODEL), F32),
        scratch_shapes=[pltpu.VMEM((2, TOP_K * tm, D_MODEL), F32), pltpu.SemaphoreType.DMA((2,))],
        compiler_params=_cparams(("arbitrary",)),
        name="moe_combine_ln3",
    )(dest3, dest3, y_sorted, h2, gates, g, b)


def _dispatch_plan(top_e, n_tok):
    rows = MOE_ROWS
    n_assign = n_tok * TOP_K
    e_flat = top_e.reshape(n_assign)
    onehot = (e_flat[:, None] == jnp.arange(N_EXPERTS, dtype=jnp.int32)[None, :]).astype(jnp.int32)
    csum = jnp.cumsum(onehot, axis=0)
    pos = jnp.sum(csum * onehot, axis=1) - 1
    counts = csum[-1]
    padded = (counts + rows - 1) // rows * rows
    pend = jnp.cumsum(padded)
    pstart = pend - padded
    dest = (pstart[e_flat] + pos).astype(jnp.int32)
    n_blocks = -(-n_assign // rows) + N_EXPERTS
    n_used = (pend[-1] // rows).astype(jnp.int32)
    blk_idx = jnp.arange(n_blocks, dtype=jnp.int32)
    blk_start = blk_idx * rows
    block_e = jnp.minimum(jnp.sum((pend[None, :] <= blk_start[:, None]).astype(jnp.int32), axis=1),
                          N_EXPERTS - 1)
    n_valid = jnp.clip(counts[block_e] - (blk_start - pstart[block_e]), 0, rows)
    n_valid = jnp.where(blk_idx < n_used, n_valid, 0).astype(jnp.int32)
    block_e = jnp.where(blk_idx < n_used, block_e, block_e[n_used - 1]).astype(jnp.int32)
    j = jnp.arange(MOE_SUB, dtype=jnp.int32)[None, :]
    sub_end = (counts + MOE_SUB - 1) // MOE_SUB * MOE_SUB
    pad_dest = jnp.where(counts[:, None] + j < sub_end[:, None], (pstart + counts)[:, None] + j, -1)
    pad_dest = pad_dest.astype(jnp.int32).reshape(1, 1, PAD_SLOTS)
    return dest, pad_dest, block_e, n_valid, n_used.reshape(1), n_blocks * rows


def kernel(x, mem, w_in, b_in, w_gla_a2, b_gla_a, gla_norm_g, hgrn_norm_g, hgrn_lb_logits, w_mix_o,
           w_xq, w_mem_kv, w_xo, w_router, b_router, w_gate_up, b_gate_up, w_down, b_down,
           ln1_g, ln1_b, ln2_g, ln2_b, ln3_g, ln3_b):
    batch, seq, d = x.shape
    t = batch * seq
    x2 = x.reshape(t, d)
    l = 0

    pad = D_IN_PAD - OFF_A1 - GLA_RANK
    w = w_in[l]
    w_gla = w[:, :GA1_SRC].astype(BF16)
    w_hgrn = w[:, GA1_SRC + GLA_RANK:].astype(BF16)
    w_a1 = jnp.pad(w[:, GA1_SRC:GA1_SRC + GLA_RANK].astype(BF16), ((0, 0), (0, PROJ_TN - GLA_RANK)))
    bi = b_in[l]
    b_in_p = jnp.concatenate([bi[:GA1_SRC], bi[GA1_SRC + GLA_RANK:], bi[GA1_SRC:GA1_SRC + GLA_RANK],
                              jnp.zeros((pad,), bi.dtype)])[None, :]
    proj = _in_proj(x2, w_gla, w_hgrn, w_a1, b_in_p, 1024)

    wa2p = jnp.concatenate([w_gla_a2[l], jnp.zeros((LANES - GLA_RANK, GLA_KEY), F32)], axis=0).astype(BF16)
    blk = min(512, seq)
    n_e, _, two_f = w_gate_up[l].shape
    wgu2 = w_gate_up[l].reshape(n_e * d, two_f)
    wd2 = w_down[l].reshape(n_e * D_EXPERT, d)
    nb = seq // blk
    fused_prep = (_side_rows(wgu2.shape[0], batch * (HGRN_HEADS // HGRN_HP) * nb) is not None
                  and _side_rows(wd2.shape[0], batch * GLA_HEADS * nb) is not None)
    gla_res = _gla_branch(proj, wa2p, b_gla_a[l][None, :], gla_norm_g[l][None, :], batch, seq, blk,
                          wd2 if fused_prep else None)
    hgrn_res = _hgrn_branch(proj, gla_res[0], hgrn_lb_logits, hgrn_norm_g[l][None, :], batch, seq, blk,
                            wgu2 if fused_prep else None)
    merged = hgrn_res[0]
    if fused_prep:
        wd_b, (wg_b, wu_b) = gla_res[1], hgrn_res[1:]
    else:
        wg_b, wu_b, wd_b = _wprep(wgu2, wd2)

    h1 = _out_ln(merged, w_mix_o[l].astype(BF16), x2, ln1_g[l][None, :], ln1_b[l][None, :], 512)

    kv = _matmul_bias(mem.reshape(batch * N_MEM, d), w_mem_kv[l].astype(BF16),
                      jnp.zeros((1, 2 * d), F32), 512, 512, BF16).reshape(batch, N_MEM, 2 * d)
    h2 = _xattn(h1, w_xq[l].astype(BF16), kv, w_xo[l].astype(BF16),
                ln2_g[l][None, :], ln2_b[l][None, :], seq, 512)

    w_rp = jnp.concatenate([w_router[l], jnp.zeros((d, LANES - N_EXPERTS), F32)], axis=1)
    b_rp = jnp.concatenate([b_router[l], jnp.zeros((LANES - N_EXPERTS,), F32)])[None, :]
    idx, gates = _router(h2, w_rp, b_rp, 512)

    dest, pad_dest, block_e, n_valid, n_used, n_rows = _dispatch_plan(idx[:, :TOP_K], t)
    tm_d = min(DISPATCH_TM, t)
    x_sorted = _dispatch(h2, dest.reshape(t // tm_d, 1, TOP_K * tm_d), pad_dest, n_rows)
    wg = wg_b.reshape(n_e, d, two_f // 2)
    wu = wu_b.reshape(n_e, d, two_f // 2)
    wd = wd_b.reshape(n_e, D_EXPERT, d)
    y_sorted = _moe_ffn(x_sorted, block_e, n_valid, n_used, wg, wu,
                        b_gate_up[l][:, None, 0::2], b_gate_up[l][:, None, 1::2],
                        wd, b_down[l][:, None, :])

    tm_c = min(COMBINE_TM, t)
    dest3 = dest.reshape(t // tm_c, tm_c, TOP_K).transpose(0, 2, 1).reshape(t // tm_c, 1, TOP_K * tm_c)
    out = _combine(dest3, y_sorted, h2, gates, ln3_g[l][None, :], ln3_b[l][None, :], tm_c)
    return out.reshape(batch, seq, d)
```

```python
import functools

import jax
import jax.numpy as jnp
from jax import lax
from jax.experimental import pallas as pl
from jax.experimental.pallas import tpu as pltpu

F32 = jnp.float32
BF16 = jnp.bfloat16

D_MODEL = 2048
N_MEM = 256
GLA_HEADS = 4
GLA_KEY = D_MODEL // 2
GLA_VAL = D_MODEL
GLA_DK = GLA_KEY // GLA_HEADS
GLA_DV = GLA_VAL // GLA_HEADS
GLA_RANK = 16
GLA_TAU = 16.0
HGRN_DK = 128
HGRN_HEADS = D_MODEL // HGRN_DK
HGRN_DV = D_MODEL // HGRN_HEADS
X_HEADS = 4
X_DH = D_MODEL // X_HEADS
N_EXPERTS = 32
TOP_K = 4
D_EXPERT = D_MODEL
SWIGLU_LIMIT = 7.0
SWIGLU_ALPHA = 1.702
DEPTH = 1
DEEPNORM_ALPHA = (2.0 * DEPTH) ** 0.25
NORM_EPS = 1e-5

LANES = 128
CHUNK = 64
SUB = 16
N_SUB = CHUNK // SUB
SUB_SHIFT = SUB.bit_length() - 1

OFF_GQ = 0
OFF_GK = GLA_KEY
OFF_GV = 2 * GLA_KEY
OFF_GR = OFF_GV + GLA_VAL
OFF_HQ = OFF_GR + GLA_VAL
OFF_HF = OFF_HQ + D_MODEL
OFF_HI = OFF_HF + D_MODEL
OFF_HG = OFF_HI + D_MODEL
OFF_MA = OFF_HG + D_MODEL
OFF_MB = OFF_MA + D_MODEL
OFF_A1 = OFF_MB + D_MODEL
PROJ_TN = 1024
D_IN_PAD = -(-(OFF_A1 + LANES) // PROJ_TN) * PROJ_TN
GA1_SRC = OFF_GR + GLA_VAL

VMEM_LIMIT = 56 * 1024 * 1024


def _cparams(sem):
    return pltpu.CompilerParams(dimension_semantics=sem, vmem_limit_bytes=VMEM_LIMIT)


def _dot(a, b):
    return jnp.dot(a, b, preferred_element_type=F32)


def _dot_nt(a, b):
    return lax.dot_general(a, b, (((1,), (1,)), ((), ())), preferred_element_type=F32)


def _dot_tn(a, b):
    return lax.dot_general(a, b, (((0,), (0,)), ((), ())), preferred_element_type=F32)


def _sigmoid(x):
    return 0.5 * jnp.tanh(0.5 * x) + 0.5


def _log_sigmoid(x):
    return jnp.minimum(x, 0.0) - jnp.log1p(jnp.exp(-jnp.abs(x)))


def _layer_norm(r, g, b):
    mu = jnp.mean(r, axis=-1, keepdims=True)
    c = r - mu
    var = jnp.mean(c * c, axis=-1, keepdims=True)
    return c * lax.rsqrt(var + NORM_EPS) * g + b


def _split_bf16(x):
    hi = x.astype(BF16)
    lo = (x - hi.astype(F32)).astype(BF16)
    return hi, lo


def _proj_kernel(x_ref, w_ref, b_ref, o_ref, xb_ref):
    @pl.when(pl.program_id(1) == 0)
    def _():
        xb_ref[...] = x_ref[...].astype(BF16)

    o_ref[...] = (_dot(xb_ref[...], w_ref[...]) + b_ref[...]).astype(o_ref.dtype)


def _matmul_bias(x, w, b, tm, tn, out_dtype):
    m, k = x.shape
    n = w.shape[1]
    tm = min(tm, m)
    return pl.pallas_call(
        _proj_kernel,
        grid=(m // tm, n // tn),
        in_specs=[pl.BlockSpec((tm, k), lambda i, j: (i, 0)),
                  pl.BlockSpec((k, tn), lambda i, j: (0, j)),
                  pl.BlockSpec((1, tn), lambda i, j: (0, j))],
        out_specs=pl.BlockSpec((tm, tn), lambda i, j: (i, j)),
        out_shape=jax.ShapeDtypeStruct((m, n), out_dtype),
        scratch_shapes=[pltpu.VMEM((tm, k), BF16)],
        compiler_params=_cparams(("arbitrary", "arbitrary")),
        name="matmul_bias",
    )(x, w, b)


PROJ_NG = OFF_HQ // PROJ_TN
PROJ_NH = (OFF_A1 - OFF_HQ) // PROJ_TN


def _in_proj_kernel(n_side_steps, x_ref, wg_ref, wh_ref, wa_ref, b_ref, *rest):
    if len(rest) == 4:
        side_ref, o_ref, side_o, xb_ref = rest
    else:
        o_ref, xb_ref = rest
    j = pl.program_id(1)

    @pl.when(j == 0)
    def _():
        xb_ref[...] = x_ref[...].astype(BF16)

    def tile(w_ref):
        o_ref[...] = (_dot(xb_ref[...], w_ref[...]) + b_ref[...]).astype(o_ref.dtype)

    @pl.when(j < PROJ_NG)
    def _():
        tile(wg_ref)

    @pl.when(jnp.logical_and(j >= PROJ_NG, j < PROJ_NG + PROJ_NH))
    def _():
        tile(wh_ref)

    @pl.when(j >= PROJ_NG + PROJ_NH)
    def _():
        a1 = _dot(xb_ref[...], wa_ref[...]) + b_ref[:, :LANES]
        o_ref[:, :LANES] = a1.astype(o_ref.dtype)
        o_ref[:, LANES:] = jnp.zeros((o_ref.shape[0], o_ref.shape[1] - LANES), o_ref.dtype)

    if len(rest) == 4:
        step = pl.program_id(0) * pl.num_programs(1) + j

        @pl.when(step < n_side_steps)
        def _():
            side_o[...] = side_ref[...].astype(BF16)


PROJ_SIDE_ROWS = 256


def _in_proj(x, wg, wh, wa, b, tm, w_side=None):
    m, k = x.shape
    tm = min(tm, m)
    tn = PROJ_TN
    n_col = D_IN_PAD // tn
    in_specs = [pl.BlockSpec((tm, k), lambda i, j: (i, 0)),
                pl.BlockSpec((k, tn), lambda i, j: (0, jnp.minimum(j, PROJ_NG - 1))),
                pl.BlockSpec((k, tn), lambda i, j: (0, jnp.clip(j - PROJ_NG, 0, PROJ_NH - 1))),
                pl.BlockSpec((k, LANES), lambda i, j: (0, 0)),
                pl.BlockSpec((1, tn), lambda i, j: (0, j))]
    out_specs = [pl.BlockSpec((tm, tn), lambda i, j: (i, j))]
    out_shape = [jax.ShapeDtypeStruct((m, D_IN_PAD), BF16)]
    operands = [x, wg, wh, wa, b]
    n_side_steps = 0
    if w_side is not None:
        rows = PROJ_SIDE_ROWS
        n_side_steps = w_side.shape[0] // rows
        assert n_side_steps * rows == w_side.shape[0] and n_side_steps <= (m // tm) * n_col
        last = n_side_steps - 1
        spec = pl.BlockSpec((rows, w_side.shape[1]), lambda i, j: (jnp.minimum(i * n_col + j, last), 0))
        in_specs.append(spec)
        out_specs.append(spec)
        out_shape.append(jax.ShapeDtypeStruct(w_side.shape, BF16))
        operands.append(w_side)
    return pl.pallas_call(
        functools.partial(_in_proj_kernel, n_side_steps),
        grid=(m // tm, n_col),
        in_specs=in_specs,
        out_specs=out_specs,
        out_shape=out_shape,
        scratch_shapes=[pltpu.VMEM((tm, k), BF16)],
        compiler_params=_cparams(("arbitrary", "arbitrary")),
        name="in_proj",
    )(*operands)


CUM_ROWS = CHUNK + 16


def _cum_matrix():
    r = lax.broadcasted_iota(jnp.int32, (CUM_ROWS, CHUNK), 0)
    c = lax.broadcasted_iota(jnp.int32, (CUM_ROWS, CHUNK), 1)
    within = jnp.logical_and(c <= r, (c >> SUB_SHIFT) == (r >> SUB_SHIFT))
    bound = c < (r - CHUNK) * SUB
    one = jnp.logical_or(jnp.logical_and(r < CHUNK, within), jnp.logical_and(r >= CHUNK, bound))
    return jnp.where(one, 1.0, 0.0).astype(BF16)


def _scan_block(q, k, v, lg, states, n_heads):
    rows, dkt = q.shape
    dk = dkt // n_heads
    dv = v.shape[1] // n_heads
    n_chunks = rows // CHUNK
    cmat = _cum_matrix()
    hi, lo = _split_bf16(lg)
    us, bounds = [], []
    for c in range(n_chunks):
        c2 = _dot(cmat, hi[c * CHUNK:(c + 1) * CHUNK]) + _dot(cmat, lo[c * CHUNK:(c + 1) * CHUNK])
        us.append(c2[:CHUNK])
        bounds.append(c2[CHUNK:CHUNK + 8])
    u = jnp.concatenate(us, axis=0)
    c_sub = [b[:N_SUB] for b in bounds]
    c_all = [b[N_SUB:N_SUB + 1] for b in bounds]
    sub_id = lax.broadcasted_iota(jnp.int32, (N_SUB, dkt), 0)

    def spread(tables):
        return jnp.concatenate([jnp.broadcast_to(t[j:j + 1, :], (SUB, dkt))
                                for t in tables for j in range(N_SUB)], axis=0)

    qe = q * jnp.exp(u)
    kq = k * jnp.exp(-u)
    qd = (qe * spread([jnp.exp(t) for t in c_sub])).astype(BF16)
    kd = (kq * spread([jnp.exp(a - t) for a, t in zip(c_all, c_sub)])).astype(BF16)
    row_in = lax.broadcasted_iota(jnp.int32, (rows, dkt), 0) & (CHUNK - 1)
    q_sub, r_sub = [], []
    for i in range(N_SUB):
        fac = [jnp.where(sub_id <= i, jnp.exp(t[i:i + 1, :] - t), 0.0) for t in c_sub]
        r_sub.append((kq * spread(fac)).astype(BF16))
        q_sub.append(jnp.where((row_in >> SUB_SHIFT) == i, qe, 0.0).astype(BF16))
    causal = (lax.broadcasted_iota(jnp.int32, (CHUNK, CHUNK), 1)
              <= lax.broadcasted_iota(jnp.int32, (CHUNK, CHUNK), 0))
    pairs = [(c, h) for h in range(n_heads) for c in range(n_chunks)]
    rs = lambda c: slice(c * CHUNK, (c + 1) * CHUNK)
    ks = lambda h: slice(h * dk, (h + 1) * dk)
    vs = lambda h: slice(h * dv, (h + 1) * dv)
    q_stack = [jnp.concatenate([x[:, ks(h)] for x in q_sub], axis=1) for h in range(n_heads)]
    r_stack = [jnp.concatenate([x[:, ks(h)] for x in r_sub], axis=1) for h in range(n_heads)]
    upd = {p: _dot_tn(v[rs(p[0]), vs(p[1])], kd[rs(p[0]), ks(p[1])]) for p in pairs}
    a = {p: _dot_nt(q_stack[p[1]][rs(p[0])], r_stack[p[1]][rs(p[0])]) for p in pairs}
    a = {p: jnp.where(causal, a[p], 0.0).astype(BF16) for p in pairs}
    s_in = {}
    new_states = []
    for h in range(n_heads):
        s = states[h]
        for c in range(n_chunks):
            s_in[(c, h)] = s.astype(BF16)
            s = s * jnp.exp(c_all[c][:, ks(h)]) + upd[(c, h)]
        new_states.append(s)
    inter = {p: _dot_nt(qd[rs(p[0]), ks(p[1])], s_in[p]) for p in pairs}
    intra = {p: _dot(a[p], v[rs(p[0]), vs(p[1])]) for p in pairs}
    o = jnp.concatenate([jnp.concatenate([inter[(c, h)] + intra[(c, h)] for h in range(n_heads)], axis=1)
                         for c in range(n_chunks)], axis=0)
    return o, new_states


def _group_rmsnorm(o, g, n_heads):
    dv = o.shape[1] // n_heads
    parts = []
    for h in range(n_heads):
        oh = o[:, h * dv:(h + 1) * dv]
        parts.append(oh * lax.rsqrt(jnp.mean(oh * oh, axis=-1, keepdims=True) + NORM_EPS) * g)
    return jnp.concatenate(parts, axis=1)


SPLIT_CW = 256
PREP_MAX_ROWS = 512


def _split_gate_up(wgu_ref, wg_o, wu_o):
    n_rows, two_f = wgu_ref.shape
    half = SPLIT_CW // 2
    r = lax.broadcasted_iota(jnp.int32, (SPLIT_CW, SPLIT_CW), 0)
    c = lax.broadcasted_iota(jnp.int32, (SPLIT_CW, SPLIT_CW), 1)
    pick = jnp.logical_or(jnp.logical_and(c < half, r == 2 * c),
                          jnp.logical_and(c >= half, r == 2 * (c - half) + 1))
    sel = jnp.where(pick, 1.0, 0.0).astype(BF16)
    n_blocks = two_f // SPLIT_CW
    x = jnp.concatenate([wgu_ref[:, j * SPLIT_CW:(j + 1) * SPLIT_CW].astype(BF16) for j in range(n_blocks)],
                        axis=0)
    y = _dot(x, sel).astype(BF16)
    for j in range(n_blocks):
        wg_o[:, j * half:(j + 1) * half] = y[j * n_rows:(j + 1) * n_rows, :half]
        wu_o[:, j * half:(j + 1) * half] = y[j * n_rows:(j + 1) * n_rows, half:]


def _side_rows(total_rows, n_steps):
    rows = total_rows // n_steps
    ok = rows * n_steps == total_rows and 16 <= rows <= PREP_MAX_ROWS and rows % 16 == 0
    return rows if ok else None


def _add_gate_up_side(in_specs, out_specs, out_shape, operands, w_side, step, n_steps):
    wgu2, lo, hi = w_side
    two_f = wgu2.shape[1]
    rows = _side_rows(hi - lo, n_steps)
    first = lo // rows
    assert first * rows == lo
    in_specs.append(pl.BlockSpec((rows, two_f), lambda b, h, c: (first + step(b, h, c), 0)))
    for _ in range(2):
        out_specs.append(pl.BlockSpec((rows, two_f // 2), lambda b, h, c: (step(b, h, c), 0)))
        out_shape.append(jax.ShapeDtypeStruct((hi - lo, two_f // 2), BF16))
    operands.append(wgu2)


def _gla_kernel(q_ref, k_ref, v_ref, r_ref, ma_ref, a1_ref, wa2_ref, ba_ref, g_ref, *rest):
    if len(rest) == 5:
        wgu_ref, o_ref, wg_o, wu_o, s_ref = rest
    else:
        o_ref, s_ref = rest

    @pl.when(pl.program_id(2) == 0)
    def _():
        s_ref[...] = jnp.zeros_like(s_ref)

    z = _dot(a1_ref[...], wa2_ref[...]) + ba_ref[...]
    lg = _log_sigmoid(z) * (1.0 / GLA_TAU)
    q = q_ref[...].astype(F32) * (GLA_DK ** -0.5)
    k = k_ref[...].astype(F32)
    o, new_states = _scan_block(q, k, v_ref[...], lg, [s_ref[...]], 1)
    s_ref[...] = new_states[0]
    o = _group_rmsnorm(o, g_ref[...], 1)
    r = r_ref[...].astype(F32)
    gate = _sigmoid(ma_ref[...].astype(F32))
    o_ref[...] = (gate * (o * (r * _sigmoid(r)))).astype(o_ref.dtype)
    if len(rest) == 5:
        _split_gate_up(wgu_ref, wg_o, wu_o)


def _gla_branch(proj, wa2p, ba, gla_g, batch, seq, blk, w_side=None):
    nb = seq // blk
    row = lambda b, h, c: b * nb + c
    step = lambda b, h, c: (b * GLA_HEADS + h) * nb + c
    kb, vb = GLA_DK, GLA_DV
    in_specs = [
        pl.BlockSpec((blk, kb), lambda b, h, c: (row(b, h, c), OFF_GQ // kb + h)),
        pl.BlockSpec((blk, kb), lambda b, h, c: (row(b, h, c), OFF_GK // kb + h)),
        pl.BlockSpec((blk, vb), lambda b, h, c: (row(b, h, c), OFF_GV // vb + h)),
        pl.BlockSpec((blk, vb), lambda b, h, c: (row(b, h, c), OFF_GR // vb + h)),
        pl.BlockSpec((blk, vb), lambda b, h, c: (row(b, h, c), OFF_MA // vb + h)),
        pl.BlockSpec((blk, LANES), lambda b, h, c: (row(b, h, c), OFF_A1 // LANES)),
        pl.BlockSpec((LANES, kb), lambda b, h, c: (0, h)),
        pl.BlockSpec((1, kb), lambda b, h, c: (0, h)),
        pl.BlockSpec((1, vb), lambda b, h, c: (0, 0)),
    ]
    out_specs = [pl.BlockSpec((blk, vb), lambda b, h, c: (row(b, h, c), h))]
    out_shape = [jax.ShapeDtypeStruct((batch * seq, GLA_VAL), BF16)]
    operands = [proj, proj, proj, proj, proj, proj, wa2p, ba, gla_g]
    if w_side is not None:
        _add_gate_up_side(in_specs, out_specs, out_shape, operands, w_side, step, batch * GLA_HEADS * nb)
    return pl.pallas_call(
        _gla_kernel,
        grid=(batch, GLA_HEADS, nb),
        in_specs=in_specs,
        out_specs=out_specs,
        out_shape=out_shape,
        scratch_shapes=[pltpu.VMEM((vb, kb), F32)],
        compiler_params=_cparams(("arbitrary", "arbitrary", "arbitrary")),
        name="gla_scan",
    )(*operands)


HGRN_HP = 2


def _hgrn_kernel(q_ref, f_ref, i_ref, g_ref, mb_ref, gla_ref, lbl_ref, ng_ref, *rest):
    if len(rest) == 5:
        wgu_ref, o_ref, wg_o, wu_o, s_ref = rest
    else:
        o_ref, s_ref = rest

    @pl.when(pl.program_id(2) == 0)
    def _():
        s_ref[...] = jnp.zeros_like(s_ref)

    lbl = lbl_ref[...].astype(F32)
    e = jnp.exp(lbl - jnp.max(lbl, axis=0, keepdims=True))
    lb = e[0:1, :] / jnp.sum(e, axis=0, keepdims=True)

    hq = q_ref[...].astype(F32)
    hf = f_ref[...].astype(F32)
    q = hq * _sigmoid(hq)
    th = jnp.tanh(0.5 * hf)
    half = 0.5 * (1.0 - lb)
    lg = jnp.log(lb + half * (1.0 + th))
    k = half * (1.0 - th)
    o, new_states = _scan_block(q, k, i_ref[...], lg, [s_ref[h] for h in range(HGRN_HP)], HGRN_HP)
    for h in range(HGRN_HP):
        s_ref[h] = new_states[h]
    o = _group_rmsnorm(o, ng_ref[...], HGRN_HP)
    o = o * _sigmoid(g_ref[...].astype(F32))
    merged = gla_ref[...].astype(F32) + _sigmoid(mb_ref[...].astype(F32)) * o
    o_ref[...] = merged.astype(o_ref.dtype)
    if len(rest) == 5:
        _split_gate_up(wgu_ref, wg_o, wu_o)


def _hgrn_branch(proj, gla_out, lb_logits, hgrn_g, batch, seq, blk, w_side=None):
    nb = seq // blk
    w = HGRN_HP * HGRN_DK
    n_groups = HGRN_HEADS // HGRN_HP
    row = lambda b, h, c: b * nb + c
    step = lambda b, h, c: (b * n_groups + h) * nb + c
    in_specs = [
        pl.BlockSpec((blk, w), lambda b, h, c: (row(b, h, c), OFF_HQ // w + h)),
        pl.BlockSpec((blk, w), lambda b, h, c: (row(b, h, c), OFF_HF // w + h)),
        pl.BlockSpec((blk, w), lambda b, h, c: (row(b, h, c), OFF_HI // w + h)),
        pl.BlockSpec((blk, w), lambda b, h, c: (row(b, h, c), OFF_HG // w + h)),
        pl.BlockSpec((blk, w), lambda b, h, c: (row(b, h, c), OFF_MB // w + h)),
        pl.BlockSpec((blk, w), lambda b, h, c: (row(b, h, c), h)),
        pl.BlockSpec((DEPTH + 1, w), lambda b, h, c: (0, h)),
        pl.BlockSpec((1, HGRN_DV), lambda b, h, c: (0, 0)),
    ]
    out_specs = [pl.BlockSpec((blk, w), lambda b, h, c: (row(b, h, c), h))]
    out_shape = [jax.ShapeDtypeStruct((batch * seq, D_MODEL), BF16)]
    operands = [proj, proj, proj, proj, proj, gla_out, lb_logits, hgrn_g]
    if w_side is not None:
        _add_gate_up_side(in_specs, out_specs, out_shape, operands, w_side, step, batch * n_groups * nb)
    return pl.pallas_call(
        _hgrn_kernel,
        grid=(batch, n_groups, nb),
        in_specs=in_specs,
        out_specs=out_specs,
        out_shape=out_shape,
        scratch_shapes=[pltpu.VMEM((HGRN_HP, HGRN_DV, HGRN_DK), F32)],
        compiler_params=_cparams(("arbitrary", "arbitrary", "arbitrary")),
        name="hgrn_scan",
    )(*operands)


def _out_ln_kernel(m_ref, w_ref, x_ref, g_ref, b_ref, o_ref):
    y = _dot(m_ref[...], w_ref[...])
    o_ref[...] = _layer_norm(DEEPNORM_ALPHA * x_ref[...] + y, g_ref[...], b_ref[...])


def _resident(shape):
    return pl.BlockSpec(shape, lambda *_: (0,) * len(shape), pipeline_mode=pl.Buffered(1))


def _out_ln(merged, w_o, x2, g, b, tm):
    t = merged.shape[0]
    tm = min(tm, t)
    return pl.pallas_call(
        _out_ln_kernel,
        grid=(t // tm,),
        in_specs=[pl.BlockSpec((tm, D_MODEL), lambda i: (i, 0)),
                  _resident((D_MODEL, D_MODEL)),
                  pl.BlockSpec((tm, D_MODEL), lambda i: (i, 0)),
                  _resident((1, D_MODEL)), _resident((1, D_MODEL))],
        out_specs=pl.BlockSpec((tm, D_MODEL), lambda i: (i, 0)),
        out_shape=jax.ShapeDtypeStruct((t, D_MODEL), F32),
        compiler_params=_cparams(("arbitrary",)),
        name="wo_ln1",
    )(merged, w_o, x2, g, b)


def _xattn_kernel(h_ref, wq_ref, kv_ref, wo_ref, g_ref, b_ref, o_ref):
    h = h_ref[...]
    q = _dot(h.astype(BF16), wq_ref[...]).astype(BF16)
    outs = []
    for hh in range(X_HEADS):
        sl = slice(hh * X_DH, (hh + 1) * X_DH)
        kh = kv_ref[0, :, sl]
        vh = kv_ref[0, :, D_MODEL + hh * X_DH:D_MODEL + (hh + 1) * X_DH]
        s = _dot_nt(q[:, sl], kh) * (X_DH ** -0.5)
        p = jnp.exp(s - jnp.max(s, axis=-1, keepdims=True))
        l = jnp.sum(p, axis=-1, keepdims=True)
        outs.append((_dot(p.astype(BF16), vh) * (1.0 / l)).astype(BF16))
    o = jnp.concatenate(outs, axis=1)
    y = _dot(o, wo_ref[...])
    o_ref[...] = _layer_norm(DEEPNORM_ALPHA * h + y, g_ref[...], b_ref[...])


def _xattn(h1, w_xq, kv, w_xo, g, b, seq, tm):
    t = h1.shape[0]
    tm = min(tm, seq)
    per_b = seq // tm
    return pl.pallas_call(
        _xattn_kernel,
        grid=(t // tm,),
        in_specs=[pl.BlockSpec((tm, D_MODEL), lambda i: (i, 0)),
                  _resident((D_MODEL, D_MODEL)),
                  pl.BlockSpec((1, N_MEM, 2 * D_MODEL), lambda i: (i // per_b, 0, 0)),
                  _resident((D_MODEL, D_MODEL)),
                  _resident((1, D_MODEL)), _resident((1, D_MODEL))],
        out_specs=pl.BlockSpec((tm, D_MODEL), lambda i: (i, 0)),
        out_shape=jax.ShapeDtypeStruct((t, D_MODEL), F32),
        compiler_params=_cparams(("arbitrary",)),
        name="xattn_ln2",
    )(h1, w_xq, kv, w_xo, g, b)


def _router_kernel(h_ref, w_ref, b_ref, idx_ref, gate_ref):
    hh, hl = _split_bf16(h_ref[...])
    wh, wl = _split_bf16(w_ref[...])
    logits = _dot(hh, wh) + _dot(hh, wl) + _dot(hl, wh) + b_ref[...]
    lane = lax.broadcasted_iota(jnp.int32, logits.shape, 1)
    neg = jnp.float32(-jnp.inf)
    logits = jnp.where(lane < N_EXPERTS, logits, neg)
    idx_out = jnp.zeros(logits.shape, jnp.int32)
    tops = []
    for k in range(TOP_K):
        m = jnp.max(logits, axis=-1, keepdims=True)
        idx = jnp.min(jnp.where(logits == m, lane, LANES), axis=-1, keepdims=True)
        idx_out = jnp.where(lane == k, idx, idx_out)
        logits = jnp.where(lane == idx, neg, logits)
        tops.append(m)
    ex = [jnp.exp(m - tops[0]) for m in tops]
    den = ex[0] + ex[1] + ex[2] + ex[3]
    gate_out = jnp.zeros(logits.shape, F32)
    for k in range(TOP_K):
        gate_out = jnp.where(lane == k, ex[k] / den, gate_out)
    idx_ref[...] = idx_out
    gate_ref[...] = gate_out


def _router(h2, w_rp, b_rp, tm):
    t = h2.shape[0]
    tm = min(tm, t)
    return pl.pallas_call(
        _router_kernel,
        grid=(t // tm,),
        in_specs=[pl.BlockSpec((tm, D_MODEL), lambda i: (i, 0)),
                  _resident((D_MODEL, LANES)), _resident((1, LANES))],
        out_specs=[pl.BlockSpec((tm, LANES), lambda i: (i, 0)),
                   pl.BlockSpec((tm, LANES), lambda i: (i, 0))],
        out_shape=[jax.ShapeDtypeStruct((t, LANES), jnp.int32),
                   jax.ShapeDtypeStruct((t, LANES), F32)],
        compiler_params=_cparams(("arbitrary",)),
        name="router_top4",
    )(h2, w_rp, b_rp)


DISPATCH_TM = 256
ISSUE_UNROLL = 8
MOE_ROWS = 1024
MOE_SUB = 256
PAD_SLOTS = N_EXPERTS * MOE_SUB


def _dispatch_kernel(dest_ref, pad_ref, h_ref, xs_hbm, stage_ref, zero_ref, sem, zsem):
    i = pl.program_id(0)
    n_tiles = pl.num_programs(0) - 1
    tm = h_ref.shape[0]
    n = TOP_K * tm

    @pl.when(i < n_tiles)
    def _():
        slot = i % 2
        stage_ref[slot] = h_ref[...]

        def issue(j8, c):
            for u in range(ISSUE_UNROLL):
                j = j8 * ISSUE_UNROLL + u
                tok = j8 * (ISSUE_UNROLL // TOP_K) + u // TOP_K
                pltpu.make_async_copy(stage_ref.at[slot, pl.ds(tok, 1), :],
                                      xs_hbm.at[pl.ds(dest_ref[0, 0, j], 1), :],
                                      sem.at[slot]).start(priority=u % 2)
            return c
        lax.fori_loop(0, n // ISSUE_UNROLL, issue, 0)

    @pl.when(i >= 1)
    def _():
        for _ in range(TOP_K):
            pltpu.make_async_copy(stage_ref.at[0, pl.ds(0, tm), :], xs_hbm.at[pl.ds(0, tm), :],
                                  sem.at[(i + 1) % 2]).wait()

    @pl.when(i == n_tiles)
    def _():
        zero_ref[...] = jnp.zeros_like(zero_ref)

        def pad_copy(j):
            return pltpu.make_async_copy(zero_ref.at[pl.ds(0, 1), :],
                                         xs_hbm.at[pl.ds(pad_ref[0, 0, j], 1), :], zsem)

        def issue(j, c):
            @pl.when(pad_ref[0, 0, j] >= 0)
            def _():
                pad_copy(j).start()
            return c
        lax.fori_loop(0, PAD_SLOTS, issue, 0)

        def drain(j, c):
            @pl.when(pad_ref[0, 0, j] >= 0)
            def _():
                pad_copy(j).wait()
            return c
        lax.fori_loop(0, PAD_SLOTS, drain, 0)


def _dispatch(h2, dest_tiles, pad_dest, n_rows):
    t = h2.shape[0]
    tm = min(DISPATCH_TM, t)
    n_tiles = t // tm
    last = n_tiles - 1
    return pl.pallas_call(
        _dispatch_kernel,
        grid=(n_tiles + 1,),
        in_specs=[pl.BlockSpec((1, 1, TOP_K * tm), lambda i: (jnp.minimum(i, last), 0, 0),
                               memory_space=pltpu.SMEM),
                  pl.BlockSpec((1, 1, PAD_SLOTS), lambda i: (0, 0, 0), memory_space=pltpu.SMEM),
                  pl.BlockSpec((tm, D_MODEL), lambda i: (jnp.minimum(i, last), 0))],
        out_specs=pl.BlockSpec(memory_space=pl.ANY),
        out_shape=jax.ShapeDtypeStruct((n_rows, D_MODEL), F32),
        scratch_shapes=[pltpu.VMEM((2, tm, D_MODEL), F32), pltpu.VMEM((8, D_MODEL), F32),
                        pltpu.SemaphoreType.DMA((2,)), pltpu.SemaphoreType.DMA(())],
        compiler_params=_cparams(("arbitrary",)),
        name="moe_dispatch",
    )(dest_tiles, pad_dest, h2)


PREP_TK = 256


def _wprep_kernel(wgu_ref, wd_ref, wg_o, wu_o, wd_o):
    _split_gate_up(wgu_ref, wg_o, wu_o)
    wd_o[...] = wd_ref[...].astype(BF16)


def _wprep(wgu2, wd2):
    total, two_f = wgu2.shape
    spec = lambda width: pl.BlockSpec((PREP_TK, width), lambda i: (i, 0))
    assert wd2.shape[0] == total, "gate/up and down weights are walked with one row index"
    return pl.pallas_call(
        _wprep_kernel,
        grid=(total // PREP_TK,),
        in_specs=[spec(two_f), spec(wd2.shape[1])],
        out_specs=[spec(two_f // 2), spec(two_f // 2), spec(wd2.shape[1])],
        out_shape=[jax.ShapeDtypeStruct((total, two_f // 2), BF16),
                   jax.ShapeDtypeStruct((total, two_f // 2), BF16),
                   jax.ShapeDtypeStruct(wd2.shape, BF16)],
        compiler_params=_cparams(("arbitrary",)),
        name="moe_weight_prep",
    )(wgu2, wd2)


MOE_TF = 512
MOE_NF = D_EXPERT // MOE_TF
MOE_NN = D_MODEL // MOE_TF


def _moe_kernel(n_lo, be_ref, nv_ref, nu_ref, x_ref, wg_lo, wu_lo, wg_hi, wu_hi, bg_ref, bu_ref, wd_ref, bd_ref,
                o_ref, xb_ref, act_ref):
    blk = pl.program_id(0)
    s = pl.program_id(1)
    n_valid = nv_ref[blk]
    n_rows = x_ref.shape[0]
    n_sub = n_rows // MOE_SUB
    full = n_valid == n_rows
    low = be_ref[blk] < n_lo

    def cast_x(rows):
        xb_ref[rows, :] = x_ref[rows, :].astype(BF16)

    def gate_up(rows, wg_ref, wu_ref):
        xs = xb_ref[rows, :]
        glu = jnp.minimum(_dot(xs, wg_ref[0]) + bg_ref[0], SWIGLU_LIMIT)
        lin = jnp.clip(_dot(xs, wu_ref[0]) + bu_ref[0], -SWIGLU_LIMIT, SWIGLU_LIMIT)
        act_ref[s, rows, :] = (glu * _sigmoid(SWIGLU_ALPHA * glu) * (lin + 1.0)).astype(BF16)

    def down(rows):
        act = jnp.concatenate([act_ref[f, rows, :] for f in range(MOE_NF)], axis=1)
        o_ref[rows, :] = _dot(act, wd_ref[0]) + bd_ref[0]

    def phases(rows, active):
        pl.when(jnp.logical_and(active, s == 0))(lambda: cast_x(rows))
        up_step = jnp.logical_and(active, s < MOE_NF)
        pl.when(jnp.logical_and(up_step, low))(lambda: gate_up(rows, wg_lo, wu_lo))
        pl.when(jnp.logical_and(up_step, jnp.logical_not(low)))(lambda: gate_up(rows, wg_hi, wu_hi))
        pl.when(jnp.logical_and(active, s >= MOE_NF))(lambda: down(rows))

    phases(slice(0, n_rows), full)
    for t in range(n_sub):
        rows = slice(t * MOE_SUB, (t + 1) * MOE_SUB)
        phases(rows, jnp.logical_and(t * MOE_SUB < n_valid, jnp.logical_not(full)))

        @pl.when(jnp.logical_and(t * MOE_SUB >= n_valid, s >= MOE_NF))
        def _(rows=rows):
            o_ref[rows, :] = jnp.zeros((MOE_SUB, MOE_TF), F32)


def _moe_ffn(x_sorted, block_e, n_valid, n_used, wg_lo, wu_lo, wg_hi, wu_hi, bg3, bu3, wd, bd3):
    rows = MOE_ROWS
    n_blocks = x_sorted.shape[0] // rows
    n_lo = wg_lo.shape[0]
    up = lambda s: jnp.minimum(s, MOE_NF - 1)
    down = lambda s: jnp.maximum(s - MOE_NF, 0)
    lo_spec = pl.BlockSpec((1, D_MODEL, MOE_TF),
                           lambda b, s, be, nv, nu: (jnp.minimum(be[b], n_lo - 1), 0, up(s)))
    hi_spec = pl.BlockSpec((1, D_MODEL, MOE_TF),
                           lambda b, s, be, nv, nu: (jnp.maximum(be[b] - n_lo, 0), 0, up(s)))
    grid_spec = pltpu.PrefetchScalarGridSpec(
        num_scalar_prefetch=3,
        grid=(n_blocks, MOE_NF + MOE_NN),
        in_specs=[
            pl.BlockSpec((rows, D_MODEL), lambda b, s, be, nv, nu: (jnp.minimum(b, nu[0] - 1), 0)),
            lo_spec, lo_spec, hi_spec, hi_spec,
            pl.BlockSpec((1, 1, MOE_TF), lambda b, s, be, nv, nu: (be[b], 0, up(s))),
            pl.BlockSpec((1, 1, MOE_TF), lambda b, s, be, nv, nu: (be[b], 0, up(s))),
            pl.BlockSpec((1, D_EXPERT, MOE_TF), lambda b, s, be, nv, nu: (be[b], 0, down(s))),
            pl.BlockSpec((1, 1, MOE_TF), lambda b, s, be, nv, nu: (be[b], 0, down(s))),
        ],
        out_specs=pl.BlockSpec((rows, MOE_TF), lambda b, s, be, nv, nu: (b, down(s))),
        scratch_shapes=[pltpu.VMEM((rows, D_MODEL), BF16),
                        pltpu.VMEM((MOE_NF, rows, MOE_TF), BF16)],
    )
    return pl.pallas_call(
        functools.partial(_moe_kernel, n_lo),
        grid_spec=grid_spec,
        out_shape=jax.ShapeDtypeStruct((n_blocks * rows, D_MODEL), F32),
        compiler_params=_cparams(("arbitrary", "arbitrary")),
        name="moe_ffn",
    )(block_e, n_valid, n_used, x_sorted, wg_lo, wu_lo, wg_hi, wu_hi, bg3, bu3, wd, bd3)


COMBINE_TM = 256


def _combine_kernel(dcur_ref, dnxt_ref, y_hbm, h_ref, gate_ref, g_ref, b_ref, o_ref, yb_ref, sem):
    i = pl.program_id(0)
    n_tiles = pl.num_programs(0)
    tm = h_ref.shape[0]
    n = TOP_K * tm

    def issue_tile(d_ref, slot):
        def issue(j8, c):
            for u in range(ISSUE_UNROLL):
                j = j8 * ISSUE_UNROLL + u
                pltpu.make_async_copy(y_hbm.at[pl.ds(d_ref[0, 0, j], 1), :],
                                      yb_ref.at[slot, pl.ds(j, 1), :], sem.at[slot]).start(priority=u % 2)
            return c
        lax.fori_loop(0, n // ISSUE_UNROLL, issue, 0)

    @pl.when(i == 0)
    def _():
        issue_tile(dcur_ref, 0)

    @pl.when(i + 1 < n_tiles)
    def _():
        issue_tile(dnxt_ref, (i + 1) % 2)

    slot = i % 2
    pltpu.make_async_copy(y_hbm.at[pl.ds(0, n), :], yb_ref.at[slot], sem.at[slot]).wait()
    gate = gate_ref[...]
    acc = DEEPNORM_ALPHA * h_ref[...]
    for k in range(TOP_K):
        acc = acc + gate[:, k:k + 1] * yb_ref[slot, k * tm:(k + 1) * tm, :]
    o_ref[...] = _layer_norm(acc, g_ref[...], b_ref[...])


def _combine(dest3, y_sorted, h2, gates, g, b, tm):
    t = h2.shape[0]
    n_tiles = t // tm
    return pl.pallas_call(
        _combine_kernel,
        grid=(n_tiles,),
        in_specs=[pl.BlockSpec((1, 1, TOP_K * tm), lambda i: (i, 0, 0), memory_space=pltpu.SMEM),
                  pl.BlockSpec((1, 1, TOP_K * tm), lambda i: (jnp.minimum(i + 1, n_tiles - 1), 0, 0),
                               memory_space=pltpu.SMEM),
                  pl.BlockSpec(memory_space=pl.ANY),
                  pl.BlockSpec((tm, D_MODEL), lambda i: (i, 0)),
                  pl.BlockSpec((tm, LANES), lambda i: (i, 0)),
                  _resident((1, D_MODEL)), _resident((1, D_MODEL))],
        out_specs=pl.BlockSpec((tm, D_MODEL), lambda i: (i, 0)),
        out_shape=jax.ShapeDtypeStruct((t, D_MODEL), F32),
        scratch_shapes=[pltpu.VMEM((2, TOP_K * tm, D_MODEL), F32), pltpu.SemaphoreType.DMA((2,))],
        compiler_params=_cparams(("arbitrary",)),
        name="moe_combine_ln3",
    )(dest3, dest3, y_sorted, h2, gates, g, b)


def _dispatch_plan(top_e, n_tok):
    rows = MOE_ROWS
    n_assign = n_tok * TOP_K
    e_flat = top_e.reshape(n_assign)
    onehot = (e_flat[:, None] == jnp.arange(N_EXPERTS, dtype=jnp.int32)[None, :]).astype(jnp.int32)
    csum = jnp.cumsum(onehot, axis=0)
    pos = jnp.sum(csum * onehot, axis=1) - 1
    counts = csum[-1]
    padded = (counts + rows - 1) // rows * rows
    pend = jnp.cumsum(padded)
    pstart = pend - padded
    dest = (pstart[e_flat] + pos).astype(jnp.int32)
    n_blocks = -(-n_assign // rows) + N_EXPERTS
    n_used = (pend[-1] // rows).astype(jnp.int32)
    blk_idx = jnp.arange(n_blocks, dtype=jnp.int32)
    blk_start = blk_idx * rows
    block_e = jnp.minimum(jnp.sum((pend[None, :] <= blk_start[:, None]).astype(jnp.int32), axis=1),
                          N_EXPERTS - 1)
    n_valid = jnp.clip(counts[block_e] - (blk_start - pstart[block_e]), 0, rows)
    n_valid = jnp.where(blk_idx < n_used, n_valid, 0).astype(jnp.int32)
    block_e = jnp.where(blk_idx < n_used, block_e, block_e[n_used - 1]).astype(jnp.int32)
    j = jnp.arange(MOE_SUB, dtype=jnp.int32)[None, :]
    sub_end = (counts + MOE_SUB - 1) // MOE_SUB * MOE_SUB
    pad_dest = jnp.where(counts[:, None] + j < sub_end[:, None], (pstart + counts)[:, None] + j, -1)
    pad_dest = pad_dest.astype(jnp.int32).reshape(1, 1, PAD_SLOTS)
    return dest, pad_dest, block_e, n_valid, n_used.reshape(1), n_blocks * rows


def kernel(x, mem, w_in, b_in, w_gla_a2, b_gla_a, gla_norm_g, hgrn_norm_g, hgrn_lb_logits, w_mix_o,
           w_xq, w_mem_kv, w_xo, w_router, b_router, w_gate_up, b_gate_up, w_down, b_down,
           ln1_g, ln1_b, ln2_g, ln2_b, ln3_g, ln3_b):
    batch, seq, d = x.shape
    t = batch * seq
    x2 = x.reshape(t, d)
    l = 0

    pad = D_IN_PAD - OFF_A1 - GLA_RANK
    w = w_in[l]
    w_gla = w[:, :GA1_SRC].astype(BF16)
    w_hgrn = w[:, GA1_SRC + GLA_RANK:].astype(BF16)
    w_a1 = jnp.pad(w[:, GA1_SRC:GA1_SRC + GLA_RANK].astype(BF16), ((0, 0), (0, LANES - GLA_RANK)))
    bi = b_in[l]
    b_in_p = jnp.concatenate([bi[:GA1_SRC], bi[GA1_SRC + GLA_RANK:], bi[GA1_SRC:GA1_SRC + GLA_RANK],
                              jnp.zeros((pad,), bi.dtype)])[None, :]
    n_e, _, two_f = w_gate_up[l].shape
    wgu2 = w_gate_up[l].reshape(n_e * d, two_f)
    wd2 = w_down[l].reshape(n_e * D_EXPERT, d)
    split = (n_e // 2) * d
    blk = min(512, seq)
    nb = seq // blk
    tm_p = min(1024, t)
    fused_prep = (_side_rows(split, batch * (HGRN_HEADS // HGRN_HP) * nb) is not None
                  and _side_rows(wgu2.shape[0] - split, batch * GLA_HEADS * nb) is not None
                  and wd2.shape[0] % PROJ_SIDE_ROWS == 0
                  and wd2.shape[0] // PROJ_SIDE_ROWS <= (t // tm_p) * (D_IN_PAD // PROJ_TN))
    proj_res = _in_proj(x2, w_gla, w_hgrn, w_a1, b_in_p, tm_p, wd2 if fused_prep else None)
    proj = proj_res[0]

    wa2p = jnp.concatenate([w_gla_a2[l], jnp.zeros((LANES - GLA_RANK, GLA_KEY), F32)], axis=0).astype(BF16)
    gla_res = _gla_branch(proj, wa2p, b_gla_a[l][None, :], gla_norm_g[l][None, :], batch, seq, blk,
                          (wgu2, split, wgu2.shape[0]) if fused_prep else None)
    hgrn_res = _hgrn_branch(proj, gla_res[0], hgrn_lb_logits, hgrn_norm_g[l][None, :], batch, seq, blk,
                            (wgu2, 0, split) if fused_prep else None)
    merged = hgrn_res[0]
    if fused_prep:
        wd_b = proj_res[1]
        wg_lo, wu_lo = hgrn_res[1:]
        wg_hi, wu_hi = gla_res[1:]
    else:
        wg_b, wu_b, wd_b = _wprep(wgu2, wd2)
        wg_lo, wg_hi, wu_lo, wu_hi = wg_b[:split], wg_b[split:], wu_b[:split], wu_b[split:]

    h1 = _out_ln(merged, w_mix_o[l].astype(BF16), x2, ln1_g[l][None, :], ln1_b[l][None, :], 512)

    kv = _matmul_bias(mem.reshape(batch * N_MEM, d), w_mem_kv[l].astype(BF16),
                      jnp.zeros((1, 2 * d), F32), 512, 512, BF16).reshape(batch, N_MEM, 2 * d)
    h2 = _xattn(h1, w_xq[l].astype(BF16), kv, w_xo[l].astype(BF16),
                ln2_g[l][None, :], ln2_b[l][None, :], seq, 512)

    w_rp = jnp.concatenate([w_router[l], jnp.zeros((d, LANES - N_EXPERTS), F32)], axis=1)
    b_rp = jnp.concatenate([b_router[l], jnp.zeros((LANES - N_EXPERTS,), F32)])[None, :]
    idx, gates = _router(h2, w_rp, b_rp, 512)

    dest, pad_dest, block_e, n_valid, n_used, n_rows = _dispatch_plan(idx[:, :TOP_K], t)
    tm_d = min(DISPATCH_TM, t)
    x_sorted = _dispatch(h2, dest.reshape(t // tm_d, 1, TOP_K * tm_d), pad_dest, n_rows)
    per_expert = lambda a: a.reshape(-1, d, two_f // 2)
    y_sorted = _moe_ffn(x_sorted, block_e, n_valid, n_used,
                        per_expert(wg_lo), per_expert(wu_lo), per_expert(wg_hi), per_expert(wu_hi),
                        b_gate_up[l][:, None, 0::2], b_gate_up[l][:, None, 1::2],
                        wd_b.reshape(n_e, D_EXPERT, d), b_down[l][:, None, :])

    tm_c = min(COMBINE_TM, t)
    dest3 = dest.reshape(t // tm_c, tm_c, TOP_K).transpose(0, 2, 1).reshape(t // tm_c, 1, TOP_K * tm_c)
    out = _combine(dest3, y_sorted, h2, gates, ln3_g[l][None, :], ln3_b[l][None, :], tm_c)
    return out.reshape(batch, seq, d)
```

```python
import functools

import jax
import jax.numpy as jnp
from jax import lax
from jax.experimental import pallas as pl
from jax.experimental.pallas import tpu as pltpu

F32 = jnp.float32
BF16 = jnp.bfloat16

D_MODEL = 2048
N_MEM = 256
GLA_HEADS = 4
GLA_KEY = D_MODEL // 2
GLA_VAL = D_MODEL
GLA_DK = GLA_KEY // GLA_HEADS
GLA_DV = GLA_VAL // GLA_HEADS
GLA_RANK = 16
GLA_TAU = 16.0
HGRN_DK = 128
HGRN_HEADS = D_MODEL // HGRN_DK
HGRN_DV = D_MODEL // HGRN_HEADS
X_HEADS = 4
X_DH = D_MODEL // X_HEADS
N_EXPERTS = 32
TOP_K = 4
D_EXPERT = D_MODEL
SWIGLU_LIMIT = 7.0
SWIGLU_ALPHA = 1.702
DEPTH = 1
DEEPNORM_ALPHA = (2.0 * DEPTH) ** 0.25
NORM_EPS = 1e-5

LANES = 128
CHUNK = 64
SUB = 16
N_SUB = CHUNK // SUB
SUB_SHIFT = SUB.bit_length() - 1

OFF_GQ = 0
OFF_GK = GLA_KEY
OFF_GV = 2 * GLA_KEY
OFF_GR = OFF_GV + GLA_VAL
OFF_HQ = OFF_GR + GLA_VAL
OFF_HF = OFF_HQ + D_MODEL
OFF_HI = OFF_HF + D_MODEL
OFF_HG = OFF_HI + D_MODEL
OFF_MA = OFF_HG + D_MODEL
OFF_MB = OFF_MA + D_MODEL
OFF_A1 = OFF_MB + D_MODEL
PROJ_TN = 1024
D_IN_PAD = -(-(OFF_A1 + LANES) // PROJ_TN) * PROJ_TN
GA1_SRC = OFF_GR + GLA_VAL

VMEM_LIMIT = 56 * 1024 * 1024


def _cparams(sem):
    return pltpu.CompilerParams(dimension_semantics=sem, vmem_limit_bytes=VMEM_LIMIT)


def _dot(a, b):
    return jnp.dot(a, b, preferred_element_type=F32)


def _dot_nt(a, b):
    return lax.dot_general(a, b, (((1,), (1,)), ((), ())), preferred_element_type=F32)


def _dot_tn(a, b):
    return lax.dot_general(a, b, (((0,), (0,)), ((), ())), preferred_element_type=F32)


def _sigmoid(x):
    return 0.5 * jnp.tanh(0.5 * x) + 0.5


def _log_sigmoid(x):
    return jnp.minimum(x, 0.0) - jnp.log1p(jnp.exp(-jnp.abs(x)))


def _layer_norm(r, g, b):
    mu = jnp.mean(r, axis=-1, keepdims=True)
    c = r - mu
    var = jnp.mean(c * c, axis=-1, keepdims=True)
    return c * lax.rsqrt(var + NORM_EPS) * g + b


def _split_bf16(x):
    hi = x.astype(BF16)
    lo = (x - hi.astype(F32)).astype(BF16)
    return hi, lo


def _proj_kernel(x_ref, w_ref, b_ref, o_ref, xb_ref):
    @pl.when(pl.program_id(1) == 0)
    def _():
        xb_ref[...] = x_ref[...].astype(BF16)

    o_ref[...] = (_dot(xb_ref[...], w_ref[...]) + b_ref[...]).astype(o_ref.dtype)


def _matmul_bias(x, w, b, tm, tn, out_dtype):
    m, k = x.shape
    n = w.shape[1]
    tm = min(tm, m)
    return pl.pallas_call(
        _proj_kernel,
        grid=(m // tm, n // tn),
        in_specs=[pl.BlockSpec((tm, k), lambda i, j: (i, 0)),
                  pl.BlockSpec((k, tn), lambda i, j: (0, j)),
                  pl.BlockSpec((1, tn), lambda i, j: (0, j))],
        out_specs=pl.BlockSpec((tm, tn), lambda i, j: (i, j)),
        out_shape=jax.ShapeDtypeStruct((m, n), out_dtype),
        scratch_shapes=[pltpu.VMEM((tm, k), BF16)],
        compiler_params=_cparams(("arbitrary", "arbitrary")),
        name="matmul_bias",
    )(x, w, b)


PROJ_NG = OFF_HQ // PROJ_TN
PROJ_NH = (OFF_A1 - OFF_HQ) // PROJ_TN


def _in_proj_kernel(n_side_steps, x_ref, wg_ref, wh_ref, wa_ref, b_ref, *rest):
    if len(rest) == 4:
        side_ref, o_ref, side_o, xb_ref = rest
    else:
        o_ref, xb_ref = rest
    j = pl.program_id(1)

    @pl.when(j == 0)
    def _():
        xb_ref[...] = x_ref[...].astype(BF16)

    def tile(w_ref):
        o_ref[...] = (_dot(xb_ref[...], w_ref[...]) + b_ref[...]).astype(o_ref.dtype)

    @pl.when(j < PROJ_NG)
    def _():
        tile(wg_ref)

    @pl.when(jnp.logical_and(j >= PROJ_NG, j < PROJ_NG + PROJ_NH))
    def _():
        tile(wh_ref)

    @pl.when(j >= PROJ_NG + PROJ_NH)
    def _():
        a1 = _dot(xb_ref[...], wa_ref[...]) + b_ref[:, :LANES]
        o_ref[:, :LANES] = a1.astype(o_ref.dtype)
        o_ref[:, LANES:] = jnp.zeros((o_ref.shape[0], o_ref.shape[1] - LANES), o_ref.dtype)

    if len(rest) == 4:
        step = pl.program_id(0) * pl.num_programs(1) + j

        @pl.when(step < n_side_steps)
        def _():
            side_o[...] = side_ref[...].astype(BF16)


PROJ_SIDE_ROWS = 256


def _in_proj(x, wg, wh, wa, b, tm, w_side=None):
    m, k = x.shape
    tm = min(tm, m)
    tn = PROJ_TN
    n_col = D_IN_PAD // tn
    in_specs = [pl.BlockSpec((tm, k), lambda i, j: (i, 0)),
                pl.BlockSpec((k, tn), lambda i, j: (0, jnp.minimum(j, PROJ_NG - 1))),
                pl.BlockSpec((k, tn), lambda i, j: (0, jnp.clip(j - PROJ_NG, 0, PROJ_NH - 1))),
                pl.BlockSpec((k, LANES), lambda i, j: (0, 0)),
                pl.BlockSpec((1, tn), lambda i, j: (0, j))]
    out_specs = [pl.BlockSpec((tm, tn), lambda i, j: (i, j))]
    out_shape = [jax.ShapeDtypeStruct((m, D_IN_PAD), BF16)]
    operands = [x, wg, wh, wa, b]
    n_side_steps = 0
    if w_side is not None:
        rows = PROJ_SIDE_ROWS
        n_side_steps = w_side.shape[0] // rows
        assert n_side_steps * rows == w_side.shape[0] and n_side_steps <= (m // tm) * n_col
        last = n_side_steps - 1
        spec = pl.BlockSpec((rows, w_side.shape[1]), lambda i, j: (jnp.minimum(i * n_col + j, last), 0))
        in_specs.append(spec)
        out_specs.append(spec)
        out_shape.append(jax.ShapeDtypeStruct(w_side.shape, BF16))
        operands.append(w_side)
    return pl.pallas_call(
        functools.partial(_in_proj_kernel, n_side_steps),
        grid=(m // tm, n_col),
        in_specs=in_specs,
        out_specs=out_specs,
        out_shape=out_shape,
        scratch_shapes=[pltpu.VMEM((tm, k), BF16)],
        compiler_params=_cparams(("arbitrary", "arbitrary")),
        name="in_proj",
    )(*operands)


CUM_ROWS = CHUNK + 16


def _cum_matrix():
    r = lax.broadcasted_iota(jnp.int32, (CUM_ROWS, CHUNK), 0)
    c = lax.broadcasted_iota(jnp.int32, (CUM_ROWS, CHUNK), 1)
    within = jnp.logical_and(c <= r, (c >> SUB_SHIFT) == (r >> SUB_SHIFT))
    bound = c < (r - CHUNK) * SUB
    one = jnp.logical_or(jnp.logical_and(r < CHUNK, within), jnp.logical_and(r >= CHUNK, bound))
    return jnp.where(one, 1.0, 0.0).astype(BF16)


def _scan_block(q, k, v, lg, states, n_heads):
    rows, dkt = q.shape
    dk = dkt // n_heads
    dv = v.shape[1] // n_heads
    n_chunks = rows // CHUNK
    cmat = _cum_matrix()
    hi, lo = _split_bf16(lg)
    us, bounds = [], []
    for c in range(n_chunks):
        c2 = _dot(cmat, hi[c * CHUNK:(c + 1) * CHUNK]) + _dot(cmat, lo[c * CHUNK:(c + 1) * CHUNK])
        us.append(c2[:CHUNK])
        bounds.append(c2[CHUNK:CHUNK + 8])
    u = jnp.concatenate(us, axis=0)
    c_sub = [b[:N_SUB] for b in bounds]
    c_all = [b[N_SUB:N_SUB + 1] for b in bounds]
    sub_id = lax.broadcasted_iota(jnp.int32, (N_SUB, dkt), 0)

    def spread(tables):
        return jnp.concatenate([jnp.broadcast_to(t[j:j + 1, :], (SUB, dkt))
                                for t in tables for j in range(N_SUB)], axis=0)

    qe = q * jnp.exp(u)
    kq = k * jnp.exp(-u)
    qd = (qe * spread([jnp.exp(t) for t in c_sub])).astype(BF16)
    kd = (kq * spread([jnp.exp(a - t) for a, t in zip(c_all, c_sub)])).astype(BF16)
    row_in = lax.broadcasted_iota(jnp.int32, (rows, dkt), 0) & (CHUNK - 1)
    q_sub, r_sub = [], []
    for i in range(N_SUB):
        fac = [jnp.where(sub_id <= i, jnp.exp(t[i:i + 1, :] - t), 0.0) for t in c_sub]
        r_sub.append((kq * spread(fac)).astype(BF16))
        q_sub.append(jnp.where((row_in >> SUB_SHIFT) == i, qe, 0.0).astype(BF16))
    causal = (lax.broadcasted_iota(jnp.int32, (CHUNK, CHUNK), 1)
              <= lax.broadcasted_iota(jnp.int32, (CHUNK, CHUNK), 0))
    pairs = [(c, h) for h in range(n_heads) for c in range(n_chunks)]
    rs = lambda c: slice(c * CHUNK, (c + 1) * CHUNK)
    ks = lambda h: slice(h * dk, (h + 1) * dk)
    vs = lambda h: slice(h * dv, (h + 1) * dv)
    q_stack = [jnp.concatenate([x[:, ks(h)] for x in q_sub], axis=1) for h in range(n_heads)]
    r_stack = [jnp.concatenate([x[:, ks(h)] for x in r_sub], axis=1) for h in range(n_heads)]
    upd = {p: _dot_tn(v[rs(p[0]), vs(p[1])], kd[rs(p[0]), ks(p[1])]) for p in pairs}
    a = {p: _dot_nt(q_stack[p[1]][rs(p[0])], r_stack[p[1]][rs(p[0])]) for p in pairs}
    a = {p: jnp.where(causal, a[p], 0.0).astype(BF16) for p in pairs}
    s_in = {}
    new_states = []
    for h in range(n_heads):
        s = states[h]
        for c in range(n_chunks):
            s_in[(c, h)] = s.astype(BF16)
            s = s * jnp.exp(c_all[c][:, ks(h)]) + upd[(c, h)]
        new_states.append(s)
    inter = {p: _dot_nt(qd[rs(p[0]), ks(p[1])], s_in[p]) for p in pairs}
    intra = {p: _dot(a[p], v[rs(p[0]), vs(p[1])]) for p in pairs}
    o = jnp.concatenate([jnp.concatenate([inter[(c, h)] + intra[(c, h)] for h in range(n_heads)], axis=1)
                         for c in range(n_chunks)], axis=0)
    return o, new_states


def _group_rmsnorm(o, g, n_heads):
    dv = o.shape[1] // n_heads
    parts = []
    for h in range(n_heads):
        oh = o[:, h * dv:(h + 1) * dv]
        parts.append(oh * lax.rsqrt(jnp.mean(oh * oh, axis=-1, keepdims=True) + NORM_EPS) * g)
    return jnp.concatenate(parts, axis=1)


SPLIT_CW = 256
PREP_MAX_ROWS = 512


def _split_gate_up(wgu_ref, wg_o, wu_o):
    n_rows, two_f = wgu_ref.shape
    half = SPLIT_CW // 2
    r = lax.broadcasted_iota(jnp.int32, (SPLIT_CW, SPLIT_CW), 0)
    c = lax.broadcasted_iota(jnp.int32, (SPLIT_CW, SPLIT_CW), 1)
    pick = jnp.logical_or(jnp.logical_and(c < half, r == 2 * c),
                          jnp.logical_and(c >= half, r == 2 * (c - half) + 1))
    sel = jnp.where(pick, 1.0, 0.0).astype(BF16)
    n_blocks = two_f // SPLIT_CW
    x = jnp.concatenate([wgu_ref[:, j * SPLIT_CW:(j + 1) * SPLIT_CW].astype(BF16) for j in range(n_blocks)],
                        axis=0)
    y = _dot(x, sel).astype(BF16)
    for j in range(n_blocks):
        wg_o[:, j * half:(j + 1) * half] = y[j * n_rows:(j + 1) * n_rows, :half]
        wu_o[:, j * half:(j + 1) * half] = y[j * n_rows:(j + 1) * n_rows, half:]


def _side_rows(total_rows, n_steps):
    rows = total_rows // n_steps
    ok = rows * n_steps == total_rows and 16 <= rows <= PREP_MAX_ROWS and rows % 16 == 0
    return rows if ok else None


def _add_gate_up_side(in_specs, out_specs, out_shape, operands, w_side, step, n_steps):
    wgu2, lo, hi = w_side
    two_f = wgu2.shape[1]
    rows = _side_rows(hi - lo, n_steps)
    first = lo // rows
    assert first * rows == lo
    in_specs.append(pl.BlockSpec((rows, two_f), lambda b, h, c: (first + step(b, h, c), 0)))
    for _ in range(2):
        out_specs.append(pl.BlockSpec((rows, two_f // 2), lambda b, h, c: (step(b, h, c), 0)))
        out_shape.append(jax.ShapeDtypeStruct((hi - lo, two_f // 2), BF16))
    operands.append(wgu2)


def _gla_kernel(q_ref, k_ref, v_ref, r_ref, ma_ref, a1_ref, wa2_ref, ba_ref, g_ref, *rest):
    if len(rest) == 5:
        wgu_ref, o_ref, wg_o, wu_o, s_ref = rest
    else:
        o_ref, s_ref = rest

    @pl.when(pl.program_id(2) == 0)
    def _():
        s_ref[...] = jnp.zeros_like(s_ref)

    z = _dot(a1_ref[...], wa2_ref[...]) + ba_ref[...]
    lg = _log_sigmoid(z) * (1.0 / GLA_TAU)
    q = q_ref[...].astype(F32) * (GLA_DK ** -0.5)
    k = k_ref[...].astype(F32)
    o, new_states = _scan_block(q, k, v_ref[...], lg, [s_ref[...]], 1)
    s_ref[...] = new_states[0]
    o = _group_rmsnorm(o, g_ref[...], 1)
    r = r_ref[...].astype(F32)
    gate = _sigmoid(ma_ref[...].astype(F32))
    o_ref[...] = (gate * (o * (r * _sigmoid(r)))).astype(o_ref.dtype)
    if len(rest) == 5:
        _split_gate_up(wgu_ref, wg_o, wu_o)


def _gla_branch(proj, wa2p, ba, gla_g, batch, seq, blk, w_side=None):
    nb = seq // blk
    row = lambda b, h, c: b * nb + c
    step = lambda b, h, c: (b * GLA_HEADS + h) * nb + c
    kb, vb = GLA_DK, GLA_DV
    in_specs = [
        pl.BlockSpec((blk, kb), lambda b, h, c: (row(b, h, c), OFF_GQ // kb + h)),
        pl.BlockSpec((blk, kb), lambda b, h, c: (row(b, h, c), OFF_GK // kb + h)),
        pl.BlockSpec((blk, vb), lambda b, h, c: (row(b, h, c), OFF_GV // vb + h)),
        pl.BlockSpec((blk, vb), lambda b, h, c: (row(b, h, c), OFF_GR // vb + h)),
        pl.BlockSpec((blk, vb), lambda b, h, c: (row(b, h, c), OFF_MA // vb + h)),
        pl.BlockSpec((blk, LANES), lambda b, h, c: (row(b, h, c), OFF_A1 // LANES)),
        pl.BlockSpec((LANES, kb), lambda b, h, c: (0, h)),
        pl.BlockSpec((1, kb), lambda b, h, c: (0, h)),
        pl.BlockSpec((1, vb), lambda b, h, c: (0, 0)),
    ]
    out_specs = [pl.BlockSpec((blk, vb), lambda b, h, c: (row(b, h, c), h))]
    out_shape = [jax.ShapeDtypeStruct((batch * seq, GLA_VAL), BF16)]
    operands = [proj, proj, proj, proj, proj, proj, wa2p, ba, gla_g]
    if w_side is not None:
        _add_gate_up_side(in_specs, out_specs, out_shape, operands, w_side, step, batch * GLA_HEADS * nb)
    return pl.pallas_call(
        _gla_kernel,
        grid=(batch, GLA_HEADS, nb),
        in_specs=in_specs,
        out_specs=out_specs,
        out_shape=out_shape,
        scratch_shapes=[pltpu.VMEM((vb, kb), F32)],
        compiler_params=_cparams(("arbitrary", "arbitrary", "arbitrary")),
        name="gla_scan",
    )(*operands)


HGRN_HP = 2


def _hgrn_kernel(q_ref, f_ref, i_ref, g_ref, mb_ref, gla_ref, lbl_ref, ng_ref, *rest):
    if len(rest) == 5:
        wgu_ref, o_ref, wg_o, wu_o, s_ref = rest
    else:
        o_ref, s_ref = rest

    @pl.when(pl.program_id(2) == 0)
    def _():
        s_ref[...] = jnp.zeros_like(s_ref)

    lbl = lbl_ref[...].astype(F32)
    e = jnp.exp(lbl - jnp.max(lbl, axis=0, keepdims=True))
    lb = e[0:1, :] / jnp.sum(e, axis=0, keepdims=True)

    hq = q_ref[...].astype(F32)
    hf = f_ref[...].astype(F32)
    q = hq * _sigmoid(hq)
    th = jnp.tanh(0.5 * hf)
    half = 0.5 * (1.0 - lb)
    lg = jnp.log(lb + half * (1.0 + th))
    k = half * (1.0 - th)
    o, new_states = _scan_block(q, k, i_ref[...], lg, [s_ref[h] for h in range(HGRN_HP)], HGRN_HP)
    for h in range(HGRN_HP):
        s_ref[h] = new_states[h]
    o = _group_rmsnorm(o, ng_ref[...], HGRN_HP)
    o = o * _sigmoid(g_ref[...].astype(F32))
    merged = gla_ref[...].astype(F32) + _sigmoid(mb_ref[...].astype(F32)) * o
    o_ref[...] = merged.astype(o_ref.dtype)
    if len(rest) == 5:
        _split_gate_up(wgu_ref, wg_o, wu_o)


def _hgrn_branch(proj, gla_out, lb_logits, hgrn_g, batch, seq, blk, w_side=None):
    nb = seq // blk
    w = HGRN_HP * HGRN_DK
    n_groups = HGRN_HEADS // HGRN_HP
    row = lambda b, h, c: b * nb + c
    step = lambda b, h, c: (b * n_groups + h) * nb + c
    in_specs = [
        pl.BlockSpec((blk, w), lambda b, h, c: (row(b, h, c), OFF_HQ // w + h)),
        pl.BlockSpec((blk, w), lambda b, h, c: (row(b, h, c), OFF_HF // w + h)),
        pl.BlockSpec((blk, w), lambda b, h, c: (row(b, h, c), OFF_HI // w + h)),
        pl.BlockSpec((blk, w), lambda b, h, c: (row(b, h, c), OFF_HG // w + h)),
        pl.BlockSpec((blk, w), lambda b, h, c: (row(b, h, c), OFF_MB // w + h)),
        pl.BlockSpec((blk, w), lambda b, h, c: (row(b, h, c), h)),
        pl.BlockSpec((DEPTH + 1, w), lambda b, h, c: (0, h)),
        pl.BlockSpec((1, HGRN_DV), lambda b, h, c: (0, 0)),
    ]
    out_specs = [pl.BlockSpec((blk, w), lambda b, h, c: (row(b, h, c), h))]
    out_shape = [jax.ShapeDtypeStruct((batch * seq, D_MODEL), BF16)]
    operands = [proj, proj, proj, proj, proj, gla_out, lb_logits, hgrn_g]
    if w_side is not None:
        _add_gate_up_side(in_specs, out_specs, out_shape, operands, w_side, step, batch * n_groups * nb)
    return pl.pallas_call(
        _hgrn_kernel,
        grid=(batch, n_groups, nb),
        in_specs=in_specs,
        out_specs=out_specs,
        out_shape=out_shape,
        scratch_shapes=[pltpu.VMEM((HGRN_HP, HGRN_DV, HGRN_DK), F32)],
        compiler_params=_cparams(("arbitrary", "arbitrary", "arbitrary")),
        name="hgrn_scan",
    )(*operands)


def _out_ln_kernel(m_ref, w_ref, x_ref, g_ref, b_ref, o_ref):
    y = _dot(m_ref[...], w_ref[...])
    o_ref[...] = _layer_norm(DEEPNORM_ALPHA * x_ref[...] + y, g_ref[...], b_ref[...])


def _resident(shape):
    return pl.BlockSpec(shape, lambda *_: (0,) * len(shape), pipeline_mode=pl.Buffered(1))


def _out_ln(merged, w_o, x2, g, b, tm):
    t = merged.shape[0]
    tm = min(tm, t)
    return pl.pallas_call(
        _out_ln_kernel,
        grid=(t // tm,),
        in_specs=[pl.BlockSpec((tm, D_MODEL), lambda i: (i, 0)),
                  _resident((D_MODEL, D_MODEL)),
                  pl.BlockSpec((tm, D_MODEL), lambda i: (i, 0)),
                  _resident((1, D_MODEL)), _resident((1, D_MODEL))],
        out_specs=pl.BlockSpec((tm, D_MODEL), lambda i: (i, 0)),
        out_shape=jax.ShapeDtypeStruct((t, D_MODEL), F32),
        compiler_params=_cparams(("arbitrary",)),
        name="wo_ln1",
    )(merged, w_o, x2, g, b)


def _xattn_kernel(h_ref, wq_ref, kv_ref, wo_ref, g_ref, b_ref, o_ref):
    h = h_ref[...]
    q = _dot(h.astype(BF16), wq_ref[...]).astype(BF16)
    outs = []
    for hh in range(X_HEADS):
        sl = slice(hh * X_DH, (hh + 1) * X_DH)
        kh = kv_ref[0, :, sl]
        vh = kv_ref[0, :, D_MODEL + hh * X_DH:D_MODEL + (hh + 1) * X_DH]
        s = _dot_nt(q[:, sl], kh) * (X_DH ** -0.5)
        p = jnp.exp(s - jnp.max(s, axis=-1, keepdims=True))
        l = jnp.sum(p, axis=-1, keepdims=True)
        outs.append((_dot(p.astype(BF16), vh) * (1.0 / l)).astype(BF16))
    o = jnp.concatenate(outs, axis=1)
    y = _dot(o, wo_ref[...])
    o_ref[...] = _layer_norm(DEEPNORM_ALPHA * h + y, g_ref[...], b_ref[...])


def _xattn(h1, w_xq, kv, w_xo, g, b, seq, tm):
    t = h1.shape[0]
    tm = min(tm, seq)
    per_b = seq // tm
    return pl.pallas_call(
        _xattn_kernel,
        grid=(t // tm,),
        in_specs=[pl.BlockSpec((tm, D_MODEL), lambda i: (i, 0)),
                  _resident((D_MODEL, D_MODEL)),
                  pl.BlockSpec((1, N_MEM, 2 * D_MODEL), lambda i: (i // per_b, 0, 0)),
                  _resident((D_MODEL, D_MODEL)),
                  _resident((1, D_MODEL)), _resident((1, D_MODEL))],
        out_specs=pl.BlockSpec((tm, D_MODEL), lambda i: (i, 0)),
        out_shape=jax.ShapeDtypeStruct((t, D_MODEL), F32),
        compiler_params=_cparams(("arbitrary",)),
        name="xattn_ln2",
    )(h1, w_xq, kv, w_xo, g, b)


def _router_kernel(h_ref, w_ref, b_ref, idx_ref, gate_ref):
    hh, hl = _split_bf16(h_ref[...])
    wh, wl = _split_bf16(w_ref[...])
    logits = _dot(hh, wh) + _dot(hh, wl) + _dot(hl, wh) + b_ref[...]
    lane = lax.broadcasted_iota(jnp.int32, logits.shape, 1)
    neg = jnp.float32(-jnp.inf)
    logits = jnp.where(lane < N_EXPERTS, logits, neg)
    idx_out = jnp.zeros(logits.shape, jnp.int32)
    tops = []
    for k in range(TOP_K):
        m = jnp.max(logits, axis=-1, keepdims=True)
        idx = jnp.min(jnp.where(logits == m, lane, LANES), axis=-1, keepdims=True)
        idx_out = jnp.where(lane == k, idx, idx_out)
        logits = jnp.where(lane == idx, neg, logits)
        tops.append(m)
    ex = [jnp.exp(m - tops[0]) for m in tops]
    den = ex[0] + ex[1] + ex[2] + ex[3]
    gate_out = jnp.zeros(logits.shape, F32)
    for k in range(TOP_K):
        gate_out = jnp.where(lane == k, ex[k] / den, gate_out)
    idx_ref[...] = idx_out
    gate_ref[...] = gate_out


def _router(h2, w_rp, b_rp, tm):
    t = h2.shape[0]
    tm = min(tm, t)
    return pl.pallas_call(
        _router_kernel,
        grid=(t // tm,),
        in_specs=[pl.BlockSpec((tm, D_MODEL), lambda i: (i, 0)),
                  _resident((D_MODEL, LANES)), _resident((1, LANES))],
        out_specs=[pl.BlockSpec((tm, LANES), lambda i: (i, 0)),
                   pl.BlockSpec((tm, LANES), lambda i: (i, 0))],
        out_shape=[jax.ShapeDtypeStruct((t, LANES), jnp.int32),
                   jax.ShapeDtypeStruct((t, LANES), F32)],
        compiler_params=_cparams(("arbitrary",)),
        name="router_top4",
    )(h2, w_rp, b_rp)


DISPATCH_TM = 256
ISSUE_UNROLL = 8
MOE_ROWS = 1024
MOE_SUB = 256
PAD_SLOTS = N_EXPERTS * MOE_SUB


def _dispatch_kernel(dest_ref, pad_ref, h_ref, xs_hbm, stage_ref, zero_ref, sem, zsem):
    i = pl.program_id(0)
    n_tiles = pl.num_programs(0) - 1
    tm = h_ref.shape[0]
    n = TOP_K * tm

    @pl.when(i < n_tiles)
    def _():
        slot = i % 2
        stage_ref[slot] = h_ref[...]

        def issue(j8, c):
            for u in range(ISSUE_UNROLL):
                j = j8 * ISSUE_UNROLL + u
                tok = j8 * (ISSUE_UNROLL // TOP_K) + u // TOP_K
                pltpu.make_async_copy(stage_ref.at[slot, pl.ds(tok, 1), :],
                                      xs_hbm.at[pl.ds(dest_ref[0, 0, j], 1), :],
                                      sem.at[slot]).start(priority=u % 2)
            return c
        lax.fori_loop(0, n // ISSUE_UNROLL, issue, 0)

    @pl.when(i >= 1)
    def _():
        for _ in range(TOP_K):
            pltpu.make_async_copy(stage_ref.at[0, pl.ds(0, tm), :], xs_hbm.at[pl.ds(0, tm), :],
                                  sem.at[(i + 1) % 2]).wait()

    @pl.when(i == n_tiles)
    def _():
        zero_ref[...] = jnp.zeros_like(zero_ref)

        def pad_copy(j):
            return pltpu.make_async_copy(zero_ref.at[pl.ds(0, 1), :],
                                         xs_hbm.at[pl.ds(pad_ref[0, 0, j], 1), :], zsem)

        def issue(j, c):
            @pl.when(pad_ref[0, 0, j] >= 0)
            def _():
                pad_copy(j).start()
            return c
        lax.fori_loop(0, PAD_SLOTS, issue, 0)

        def drain(j, c):
            @pl.when(pad_ref[0, 0, j] >= 0)
            def _():
                pad_copy(j).wait()
            return c
        lax.fori_loop(0, PAD_SLOTS, drain, 0)


def _dispatch(h2, dest_tiles, pad_dest, n_rows):
    t = h2.shape[0]
    tm = min(DISPATCH_TM, t)
    n_tiles = t // tm
    last = n_tiles - 1
    return pl.pallas_call(
        _dispatch_kernel,
        grid=(n_tiles + 1,),
        in_specs=[pl.BlockSpec((1, 1, TOP_K * tm), lambda i: (jnp.minimum(i, last), 0, 0),
                               memory_space=pltpu.SMEM),
                  pl.BlockSpec((1, 1, PAD_SLOTS), lambda i: (0, 0, 0), memory_space=pltpu.SMEM),
                  pl.BlockSpec((tm, D_MODEL), lambda i: (jnp.minimum(i, last), 0))],
        out_specs=pl.BlockSpec(memory_space=pl.ANY),
        out_shape=jax.ShapeDtypeStruct((n_rows, D_MODEL), F32),
        scratch_shapes=[pltpu.VMEM((2, tm, D_MODEL), F32), pltpu.VMEM((8, D_MODEL), F32),
                        pltpu.SemaphoreType.DMA((2,)), pltpu.SemaphoreType.DMA(())],
        compiler_params=_cparams(("arbitrary",)),
        name="moe_dispatch",
    )(dest_tiles, pad_dest, h2)


PREP_TK = 256


def _wprep_kernel(wgu_ref, wd_ref, wg_o, wu_o, wd_o):
    _split_gate_up(wgu_ref, wg_o, wu_o)
    wd_o[...] = wd_ref[...].astype(BF16)


def _wprep(wgu2, wd2):
    total, two_f = wgu2.shape
    spec = lambda width: pl.BlockSpec((PREP_TK, width), lambda i: (i, 0))
    assert wd2.shape[0] == total, "gate/up and down weights are walked with one row index"
    return pl.pallas_call(
        _wprep_kernel,
        grid=(total // PREP_TK,),
        in_specs=[spec(two_f), spec(wd2.shape[1])],
        out_specs=[spec(two_f // 2), spec(two_f // 2), spec(wd2.shape[1])],
        out_shape=[jax.ShapeDtypeStruct((total, two_f // 2), BF16),
                   jax.ShapeDtypeStruct((total, two_f // 2), BF16),
                   jax.ShapeDtypeStruct(wd2.shape, BF16)],
        compiler_params=_cparams(("arbitrary",)),
        name="moe_weight_prep",
    )(wgu2, wd2)


MOE_TF = 512
MOE_NF = D_EXPERT // MOE_TF
MOE_NN = D_MODEL // MOE_TF


def _moe_kernel(n_lo, be_ref, nv_ref, nu_ref, x_ref, wg_lo, wu_lo, wg_hi, wu_hi, bg_ref, bu_ref, wd_ref, bd_ref,
                o_ref, xb_ref, act_ref):
    blk = pl.program_id(0)
    s = pl.program_id(1)
    n_valid = nv_ref[blk]
    n_rows = x_ref.shape[0]
    n_sub = n_rows // MOE_SUB
    full = n_valid == n_rows
    low = be_ref[blk] < n_lo

    def cast_x(rows):
        xb_ref[rows, :] = x_ref[rows, :].astype(BF16)

    def gate_up(rows, wg_ref, wu_ref):
        xs = xb_ref[rows, :]
        glu = jnp.minimum(_dot(xs, wg_ref[0]) + bg_ref[0], SWIGLU_LIMIT)
        lin = jnp.clip(_dot(xs, wu_ref[0]) + bu_ref[0], -SWIGLU_LIMIT, SWIGLU_LIMIT)
        act_ref[s, rows, :] = (glu * _sigmoid(SWIGLU_ALPHA * glu) * (lin + 1.0)).astype(BF16)

    def down(rows):
        act = jnp.concatenate([act_ref[f, rows, :] for f in range(MOE_NF)], axis=1)
        o_ref[rows, :] = _dot(act, wd_ref[0]) + bd_ref[0]

    def phases(rows, active):
        pl.when(jnp.logical_and(active, s == 0))(lambda: cast_x(rows))
        up_step = jnp.logical_and(active, s < MOE_NF)
        pl.when(jnp.logical_and(up_step, low))(lambda: gate_up(rows, wg_lo, wu_lo))
        pl.when(jnp.logical_and(up_step, jnp.logical_not(low)))(lambda: gate_up(rows, wg_hi, wu_hi))
        pl.when(jnp.logical_and(active, s >= MOE_NF))(lambda: down(rows))

    phases(slice(0, n_rows), full)
    for t in range(n_sub):
        rows = slice(t * MOE_SUB, (t + 1) * MOE_SUB)
        phases(rows, jnp.logical_and(t * MOE_SUB < n_valid, jnp.logical_not(full)))

        @pl.when(jnp.logical_and(t * MOE_SUB >= n_valid, s >= MOE_NF))
        def _(rows=rows):
            o_ref[rows, :] = jnp.zeros((MOE_SUB, MOE_TF), F32)


def _moe_ffn(x_sorted, block_e, n_valid, n_used, wg_lo, wu_lo, wg_hi, wu_hi, bg3, bu3, wd, bd3):
    rows = MOE_ROWS
    n_blocks = x_sorted.shape[0] // rows
    n_lo = wg_lo.shape[0]
    up = lambda s: jnp.minimum(s, MOE_NF - 1)
    down = lambda s: jnp.maximum(s - MOE_NF, 0)
    lo_spec = pl.BlockSpec((1, D_MODEL, MOE_TF), lambda b, s, be, nv, nu: (
        jnp.minimum(be[b], n_lo - 1), 0, jnp.where(be[b] < n_lo, up(s), MOE_NF - 1)))
    hi_spec = pl.BlockSpec((1, D_MODEL, MOE_TF), lambda b, s, be, nv, nu: (
        jnp.maximum(be[b] - n_lo, 0), 0, jnp.where(be[b] >= n_lo, up(s), 0)))
    grid_spec = pltpu.PrefetchScalarGridSpec(
        num_scalar_prefetch=3,
        grid=(n_blocks, MOE_NF + MOE_NN),
        in_specs=[
            pl.BlockSpec((rows, D_MODEL), lambda b, s, be, nv, nu: (jnp.minimum(b, nu[0] - 1), 0)),
            lo_spec, lo_spec, hi_spec, hi_spec,
            pl.BlockSpec((1, 1, MOE_TF), lambda b, s, be, nv, nu: (be[b], 0, up(s))),
            pl.BlockSpec((1, 1, MOE_TF), lambda b, s, be, nv, nu: (be[b], 0, up(s))),
            pl.BlockSpec((1, D_EXPERT, MOE_TF), lambda b, s, be, nv, nu: (be[b], 0, down(s))),
            pl.BlockSpec((1, 1, MOE_TF), lambda b, s, be, nv, nu: (be[b], 0, down(s))),
        ],
        out_specs=pl.BlockSpec((rows, MOE_TF), lambda b, s, be, nv, nu: (b, down(s))),
        scratch_shapes=[pltpu.VMEM((rows, D_MODEL), BF16),
                        pltpu.VMEM((MOE_NF, rows, MOE_TF), BF16)],
    )
    return pl.pallas_call(
        functools.partial(_moe_kernel, n_lo),
        grid_spec=grid_spec,
        out_shape=jax.ShapeDtypeStruct((n_blocks * rows, D_MODEL), F32),
        compiler_params=_cparams(("arbitrary", "arbitrary")),
        name="moe_ffn",
    )(block_e, n_valid, n_used, x_sorted, wg_lo, wu_lo, wg_hi, wu_hi, bg3, bu3, wd, bd3)


COMBINE_TM = 256


def _combine_kernel(dcur_ref, dnxt_ref, y_hbm, h_ref, gate_ref, g_ref, b_ref, o_ref, yb_ref, sem):
    i = pl.program_id(0)
    n_tiles = pl.num_programs(0)
    tm = h_ref.shape[0]
    n = TOP_K * tm

    def issue_tile(d_ref, slot):
        def issue(j8, c):
            for u in range(ISSUE_UNROLL):
                j = j8 * ISSUE_UNROLL + u
                pltpu.make_async_copy(y_hbm.at[pl.ds(d_ref[0, 0, j], 1), :],
                                      yb_ref.at[slot, pl.ds(j, 1), :], sem.at[slot]).start(priority=u % 2)
            return c
        lax.fori_loop(0, n // ISSUE_UNROLL, issue, 0)

    @pl.when(i == 0)
    def _():
        issue_tile(dcur_ref, 0)

    @pl.when(i + 1 < n_tiles)
    def _():
        issue_tile(dnxt_ref, (i + 1) % 2)

    slot = i % 2
    pltpu.make_async_copy(y_hbm.at[pl.ds(0, n), :], yb_ref.at[slot], sem.at[slot]).wait()
    gate = gate_ref[...]
    acc = DEEPNORM_ALPHA * h_ref[...]
    for k in range(TOP_K):
        acc = acc + gate[:, k:k + 1] * yb_ref[slot, k * tm:(k + 1) * tm, :]
    o_ref[...] = _layer_norm(acc, g_ref[...], b_ref[...])


def _combine(dest3, y_sorted, h2, gates, g, b, tm):
    t = h2.shape[0]
    n_tiles = t // tm
    return pl.pallas_call(
        _combine_kernel,
        grid=(n_tiles,),
        in_specs=[pl.BlockSpec((1, 1, TOP_K * tm), lambda i: (i, 0, 0), memory_space=pltpu.SMEM),
                  pl.BlockSpec((1, 1, TOP_K * tm), lambda i: (jnp.minimum(i + 1, n_tiles - 1), 0, 0),
                               memory_space=pltpu.SMEM),
                  pl.BlockSpec(memory_space=pl.ANY),
                  pl.BlockSpec((tm, D_MODEL), lambda i: (i, 0)),
                  pl.BlockSpec((tm, LANES), lambda i: (i, 0)),
                  _resident((1, D_MODEL)), _resident((1, D_MODEL))],
        out_specs=pl.BlockSpec((tm, D_MODEL), lambda i: (i, 0)),
        out_shape=jax.ShapeDtypeStruct((t, D_MODEL), F32),
        scratch_shapes=[pltpu.VMEM((2, TOP_K * tm, D_MODEL), F32), pltpu.SemaphoreType.DMA((2,))],
        compiler_params=_cparams(("arbitrary",)),
        name="moe_combine_ln3",
    )(dest3, dest3, y_sorted, h2, gates, g, b)


def _dispatch_plan(top_e, n_tok):
    rows = MOE_ROWS
    n_assign = n_tok * TOP_K
    e_flat = top_e.reshape(n_assign)
    onehot = (e_flat[:, None] == jnp.arange(N_EXPERTS, dtype=jnp.int32)[None, :]).astype(jnp.int32)
    csum = jnp.cumsum(onehot, axis=0)
    pos = jnp.sum(csum * onehot, axis=1) - 1
    counts = csum[-1]
    padded = (counts + rows - 1) // rows * rows
    pend = jnp.cumsum(padded)
    pstart = pend - padded
    dest = (pstart[e_flat] + pos).astype(jnp.int32)
    n_blocks = -(-n_assign // rows) + N_EXPERTS
    n_used = (pend[-1] // rows).astype(jnp.int32)
    blk_idx = jnp.arange(n_blocks, dtype=jnp.int32)
    blk_start = blk_idx * rows
    block_e = jnp.minimum(jnp.sum((pend[None, :] <= blk_start[:, None]).astype(jnp.int32), axis=1),
                          N_EXPERTS - 1)
    n_valid = jnp.clip(counts[block_e] - (blk_start - pstart[block_e]), 0, rows)
    n_valid = jnp.where(blk_idx < n_used, n_valid, 0).astype(jnp.int32)
    block_e = jnp.where(blk_idx < n_used, block_e, block_e[n_used - 1]).astype(jnp.int32)
    j = jnp.arange(MOE_SUB, dtype=jnp.int32)[None, :]
    sub_end = (counts + MOE_SUB - 1) // MOE_SUB * MOE_SUB
    pad_dest = jnp.where(counts[:, None] + j < sub_end[:, None], (pstart + counts)[:, None] + j, -1)
    pad_dest = pad_dest.astype(jnp.int32).reshape(1, 1, PAD_SLOTS)
    return dest, pad_dest, block_e, n_valid, n_used.reshape(1), n_blocks * rows


def kernel(x, mem, w_in, b_in, w_gla_a2, b_gla_a, gla_norm_g, hgrn_norm_g, hgrn_lb_logits, w_mix_o,
           w_xq, w_mem_kv, w_xo, w_router, b_router, w_gate_up, b_gate_up, w_down, b_down,
           ln1_g, ln1_b, ln2_g, ln2_b, ln3_g, ln3_b):
    batch, seq, d = x.shape
    t = batch * seq
    x2 = x.reshape(t, d)
    l = 0

    pad = D_IN_PAD - OFF_A1 - GLA_RANK
    w = w_in[l]
    w_gla = w[:, :GA1_SRC].astype(BF16)
    w_hgrn = w[:, GA1_SRC + GLA_RANK:].astype(BF16)
    w_a1 = jnp.pad(w[:, GA1_SRC:GA1_SRC + GLA_RANK].astype(BF16), ((0, 0), (0, LANES - GLA_RANK)))
    bi = b_in[l]
    b_in_p = jnp.concatenate([bi[:GA1_SRC], bi[GA1_SRC + GLA_RANK:], bi[GA1_SRC:GA1_SRC + GLA_RANK],
                              jnp.zeros((pad,), bi.dtype)])[None, :]
    n_e, _, two_f = w_gate_up[l].shape
    wgu2 = w_gate_up[l].reshape(n_e * d, two_f)
    wd2 = w_down[l].reshape(n_e * D_EXPERT, d)
    split = (n_e // 2) * d
    blk = min(512, seq)
    nb = seq // blk
    tm_p = min(1024, t)
    fused_prep = (_side_rows(split, batch * (HGRN_HEADS // HGRN_HP) * nb) is not None
                  and _side_rows(wgu2.shape[0] - split, batch * GLA_HEADS * nb) is not None
                  and wd2.shape[0] % PROJ_SIDE_ROWS == 0
                  and wd2.shape[0] // PROJ_SIDE_ROWS <= (t // tm_p) * (D_IN_PAD // PROJ_TN))
    proj_res = _in_proj(x2, w_gla, w_hgrn, w_a1, b_in_p, tm_p, wd2 if fused_prep else None)
    proj = proj_res[0]

    wa2p = jnp.concatenate([w_gla_a2[l], jnp.zeros((LANES - GLA_RANK, GLA_KEY), F32)], axis=0).astype(BF16)
    gla_res = _gla_branch(proj, wa2p, b_gla_a[l][None, :], gla_norm_g[l][None, :], batch, seq, blk,
                          (wgu2, split, wgu2.shape[0]) if fused_prep else None)
    hgrn_res = _hgrn_branch(proj, gla_res[0], hgrn_lb_logits, hgrn_norm_g[l][None, :], batch, seq, blk,
                            (wgu2, 0, split) if fused_prep else None)
    merged = hgrn_res[0]
    if fused_prep:
        wd_b = proj_res[1]
        wg_lo, wu_lo = hgrn_res[1:]
        wg_hi, wu_hi = gla_res[1:]
    else:
        wg_b, wu_b, wd_b = _wprep(wgu2, wd2)
        wg_lo, wg_hi, wu_lo, wu_hi = wg_b[:split], wg_b[split:], wu_b[:split], wu_b[split:]

    h1 = _out_ln(merged, w_mix_o[l].astype(BF16), x2, ln1_g[l][None, :], ln1_b[l][None, :], 512)

    kv = _matmul_bias(mem.reshape(batch * N_MEM, d), w_mem_kv[l].astype(BF16),
                      jnp.zeros((1, 2 * d), F32), 512, 512, BF16).reshape(batch, N_MEM, 2 * d)
    h2 = _xattn(h1, w_xq[l].astype(BF16), kv, w_xo[l].astype(BF16),
                ln2_g[l][None, :], ln2_b[l][None, :], seq, 512)

    w_rp = jnp.concatenate([w_router[l], jnp.zeros((d, LANES - N_EXPERTS), F32)], axis=1)
    b_rp = jnp.concatenate([b_router[l], jnp.zeros((LANES - N_EXPERTS,), F32)])[None, :]
    idx, gates = _router(h2, w_rp, b_rp, 512)

    dest, pad_dest, block_e, n_valid, n_used, n_rows = _dispatch_plan(idx[:, :TOP_K], t)
    tm_d = min(DISPATCH_TM, t)
    x_sorted = _dispatch(h2, dest.reshape(t // tm_d, 1, TOP_K * tm_d), pad_dest, n_rows)
    per_expert = lambda a: a.reshape(-1, d, two_f // 2)
    y_sorted = _moe_ffn(x_sorted, block_e, n_valid, n_used,
                        per_expert(wg_lo), per_expert(wu_lo), per_expert(wg_hi), per_expert(wu_hi),
                        b_gate_up[l][:, None, 0::2], b_gate_up[l][:, None, 1::2],
                        wd_b.reshape(n_e, D_EXPERT, d), b_down[l][:, None, :])

    tm_c = min(COMBINE_TM, t)
    dest3 = dest.reshape(t // tm_c, tm_c, TOP_K).transpose(0, 2, 1).reshape(t // tm_c, 1, TOP_K * tm_c)
    out = _combine(dest3, y_sorted, h2, gates, ln3_g[l][None, :], ln3_b[l][None, :], tm_c)
    return out.reshape(batch, seq, d)
```

```python
import functools

import jax
import jax.numpy as jnp
from jax import lax
from jax.experimental import pallas as pl
from jax.experimental.pallas import tpu as pltpu

F32 = jnp.float32
BF16 = jnp.bfloat16

D_MODEL = 2048
N_MEM = 256
GLA_HEADS = 4
GLA_KEY = D_MODEL // 2
GLA_VAL = D_MODEL
GLA_DK = GLA_KEY // GLA_HEADS
GLA_DV = GLA_VAL // GLA_HEADS
GLA_RANK = 16
GLA_TAU = 16.0
HGRN_DK = 128
HGRN_HEADS = D_MODEL // HGRN_DK
HGRN_DV = D_MODEL // HGRN_HEADS
X_HEADS = 4
X_DH = D_MODEL // X_HEADS
N_EXPERTS = 32
TOP_K = 4
D_EXPERT = D_MODEL
SWIGLU_LIMIT = 7.0
SWIGLU_ALPHA = 1.702
DEPTH = 1
DEEPNORM_ALPHA = (2.0 * DEPTH) ** 0.25
NORM_EPS = 1e-5

LANES = 128
CHUNK = 64
SUB = 16
N_SUB = CHUNK // SUB
SUB_SHIFT = SUB.bit_length() - 1

OFF_GQ = 0
OFF_GK = GLA_KEY
OFF_GV = 2 * GLA_KEY
OFF_GR = OFF_GV + GLA_VAL
OFF_HQ = OFF_GR + GLA_VAL
OFF_HF = OFF_HQ + D_MODEL
OFF_HI = OFF_HF + D_MODEL
OFF_HG = OFF_HI + D_MODEL
OFF_MA = OFF_HG + D_MODEL
OFF_MB = OFF_MA + D_MODEL
OFF_A1 = OFF_MB + D_MODEL
PROJ_TN = 1024
D_IN_PAD = -(-(OFF_A1 + LANES) // PROJ_TN) * PROJ_TN
GA1_SRC = OFF_GR + GLA_VAL

VMEM_LIMIT = 56 * 1024 * 1024


def _cparams(sem):
    return pltpu.CompilerParams(dimension_semantics=sem, vmem_limit_bytes=VMEM_LIMIT)


def _dot(a, b):
    return jnp.dot(a, b, preferred_element_type=F32)


def _dot_nt(a, b):
    return lax.dot_general(a, b, (((1,), (1,)), ((), ())), preferred_element_type=F32)


def _dot_tn(a, b):
    return lax.dot_general(a, b, (((0,), (0,)), ((), ())), preferred_element_type=F32)


def _sigmoid(x):
    return 0.5 * jnp.tanh(0.5 * x) + 0.5


def _log_sigmoid(x):
    return jnp.minimum(x, 0.0) - jnp.log1p(jnp.exp(-jnp.abs(x)))


def _layer_norm(r, g, b):
    mu = jnp.mean(r, axis=-1, keepdims=True)
    c = r - mu
    var = jnp.mean(c * c, axis=-1, keepdims=True)
    return c * lax.rsqrt(var + NORM_EPS) * g + b


def _split_bf16(x):
    hi = x.astype(BF16)
    lo = (x - hi.astype(F32)).astype(BF16)
    return hi, lo


def _proj_kernel(x_ref, w_ref, b_ref, o_ref, xb_ref):
    @pl.when(pl.program_id(1) == 0)
    def _():
        xb_ref[...] = x_ref[...].astype(BF16)

    o_ref[...] = (_dot(xb_ref[...], w_ref[...]) + b_ref[...]).astype(o_ref.dtype)


def _matmul_bias(x, w, b, tm, tn, out_dtype):
    m, k = x.shape
    n = w.shape[1]
    tm = min(tm, m)
    return pl.pallas_call(
        _proj_kernel,
        grid=(m // tm, n // tn),
        in_specs=[pl.BlockSpec((tm, k), lambda i, j: (i, 0)),
                  pl.BlockSpec((k, tn), lambda i, j: (0, j)),
                  pl.BlockSpec((1, tn), lambda i, j: (0, j))],
        out_specs=pl.BlockSpec((tm, tn), lambda i, j: (i, j)),
        out_shape=jax.ShapeDtypeStruct((m, n), out_dtype),
        scratch_shapes=[pltpu.VMEM((tm, k), BF16)],
        compiler_params=_cparams(("arbitrary", "arbitrary")),
        name="matmul_bias",
    )(x, w, b)


PROJ_NG = OFF_HQ // PROJ_TN
PROJ_NH = (OFF_A1 - OFF_HQ) // PROJ_TN


def _in_proj_kernel(n_side_steps, x_ref, wg_ref, wh_ref, wa_ref, b_ref, *rest):
    if len(rest) == 4:
        side_ref, o_ref, side_o, xb_ref = rest
    else:
        o_ref, xb_ref = rest
    j = pl.program_id(1)

    @pl.when(j == 0)
    def _():
        xb_ref[...] = x_ref[...].astype(BF16)

    def tile(w_ref):
        o_ref[...] = (_dot(xb_ref[...], w_ref[...]) + b_ref[...]).astype(o_ref.dtype)

    @pl.when(j < PROJ_NG)
    def _():
        tile(wg_ref)

    @pl.when(jnp.logical_and(j >= PROJ_NG, j < PROJ_NG + PROJ_NH))
    def _():
        tile(wh_ref)

    @pl.when(j >= PROJ_NG + PROJ_NH)
    def _():
        a1 = _dot(xb_ref[...], wa_ref[...]) + b_ref[:, :LANES]
        o_ref[:, :LANES] = a1.astype(o_ref.dtype)
        o_ref[:, LANES:] = jnp.zeros((o_ref.shape[0], o_ref.shape[1] - LANES), o_ref.dtype)

    if len(rest) == 4:
        step = pl.program_id(0) * pl.num_programs(1) + j

        @pl.when(step < n_side_steps)
        def _():
            side_o[...] = side_ref[...].astype(BF16)


PROJ_SIDE_ROWS = 256


def _in_proj(x, wg, wh, wa, b, tm, w_side=None):
    m, k = x.shape
    tm = min(tm, m)
    tn = PROJ_TN
    n_col = D_IN_PAD // tn
    in_specs = [pl.BlockSpec((tm, k), lambda i, j: (i, 0)),
                pl.BlockSpec((k, tn), lambda i, j: (0, jnp.minimum(j, PROJ_NG - 1))),
                pl.BlockSpec((k, tn), lambda i, j: (0, jnp.clip(j - PROJ_NG, 0, PROJ_NH - 1))),
                pl.BlockSpec((k, LANES), lambda i, j: (0, 0)),
                pl.BlockSpec((1, tn), lambda i, j: (0, j))]
    out_specs = [pl.BlockSpec((tm, tn), lambda i, j: (i, j))]
    out_shape = [jax.ShapeDtypeStruct((m, D_IN_PAD), BF16)]
    operands = [x, wg, wh, wa, b]
    n_side_steps = 0
    if w_side is not None:
        rows = PROJ_SIDE_ROWS
        n_side_steps = w_side.shape[0] // rows
        assert n_side_steps * rows == w_side.shape[0] and n_side_steps <= (m // tm) * n_col
        last = n_side_steps - 1
        spec = pl.BlockSpec((rows, w_side.shape[1]), lambda i, j: (jnp.minimum(i * n_col + j, last), 0))
        in_specs.append(spec)
        out_specs.append(spec)
        out_shape.append(jax.ShapeDtypeStruct(w_side.shape, BF16))
        operands.append(w_side)
    return pl.pallas_call(
        functools.partial(_in_proj_kernel, n_side_steps),
        grid=(m // tm, n_col),
        in_specs=in_specs,
        out_specs=out_specs,
        out_shape=out_shape,
        scratch_shapes=[pltpu.VMEM((tm, k), BF16)],
        compiler_params=_cparams(("arbitrary", "arbitrary")),
        name="in_proj",
    )(*operands)


CUM_ROWS = CHUNK + 16


def _cum_matrix():
    r = lax.broadcasted_iota(jnp.int32, (CUM_ROWS, CHUNK), 0)
    c = lax.broadcasted_iota(jnp.int32, (CUM_ROWS, CHUNK), 1)
    within = jnp.logical_and(c <= r, (c >> SUB_SHIFT) == (r >> SUB_SHIFT))
    bound = c < (r - CHUNK) * SUB
    one = jnp.logical_or(jnp.logical_and(r < CHUNK, within), jnp.logical_and(r >= CHUNK, bound))
    return jnp.where(one, 1.0, 0.0).astype(BF16)


def _scan_block(q, k, v, lg, states, n_heads):
    rows, dkt = q.shape
    dk = dkt // n_heads
    dv = v.shape[1] // n_heads
    n_chunks = rows // CHUNK
    cmat = _cum_matrix()
    hi, lo = _split_bf16(lg)
    us, bounds = [], []
    for c in range(n_chunks):
        c2 = _dot(cmat, hi[c * CHUNK:(c + 1) * CHUNK]) + _dot(cmat, lo[c * CHUNK:(c + 1) * CHUNK])
        us.append(c2[:CHUNK])
        bounds.append(c2[CHUNK:CHUNK + 8])
    u = jnp.concatenate(us, axis=0)
    c_sub = [b[:N_SUB] for b in bounds]
    c_all = [b[N_SUB:N_SUB + 1] for b in bounds]
    sub_id = lax.broadcasted_iota(jnp.int32, (N_SUB, dkt), 0)

    def spread(tables):
        return jnp.concatenate([jnp.broadcast_to(t[j:j + 1, :], (SUB, dkt))
                                for t in tables for j in range(N_SUB)], axis=0)

    qe = q * jnp.exp(u)
    kq = k * jnp.exp(-u)
    qd = (qe * spread([jnp.exp(t) for t in c_sub])).astype(BF16)
    kd = (kq * spread([jnp.exp(a - t) for a, t in zip(c_all, c_sub)])).astype(BF16)
    row_in = lax.broadcasted_iota(jnp.int32, (rows, dkt), 0) & (CHUNK - 1)
    q_sub, r_sub = [], []
    for i in range(N_SUB):
        fac = [jnp.where(sub_id <= i, jnp.exp(t[i:i + 1, :] - t), 0.0) for t in c_sub]
        r_sub.append((kq * spread(fac)).astype(BF16))
        q_sub.append(jnp.where((row_in >> SUB_SHIFT) == i, qe, 0.0).astype(BF16))
    causal = (lax.broadcasted_iota(jnp.int32, (CHUNK, CHUNK), 1)
              <= lax.broadcasted_iota(jnp.int32, (CHUNK, CHUNK), 0))
    pairs = [(c, h) for h in range(n_heads) for c in range(n_chunks)]
    rs = lambda c: slice(c * CHUNK, (c + 1) * CHUNK)
    ks = lambda h: slice(h * dk, (h + 1) * dk)
    vs = lambda h: slice(h * dv, (h + 1) * dv)
    q_stack = [jnp.concatenate([x[:, ks(h)] for x in q_sub], axis=1) for h in range(n_heads)]
    r_stack = [jnp.concatenate([x[:, ks(h)] for x in r_sub], axis=1) for h in range(n_heads)]
    upd = {p: _dot_tn(v[rs(p[0]), vs(p[1])], kd[rs(p[0]), ks(p[1])]) for p in pairs}
    a = {p: _dot_nt(q_stack[p[1]][rs(p[0])], r_stack[p[1]][rs(p[0])]) for p in pairs}
    a = {p: jnp.where(causal, a[p], 0.0).astype(BF16) for p in pairs}
    s_in = {}
    new_states = []
    for h in range(n_heads):
        s = states[h]
        for c in range(n_chunks):
            s_in[(c, h)] = s.astype(BF16)
            s = s * jnp.exp(c_all[c][:, ks(h)]) + upd[(c, h)]
        new_states.append(s)
    inter = {p: _dot_nt(qd[rs(p[0]), ks(p[1])], s_in[p]) for p in pairs}
    intra = {p: _dot(a[p], v[rs(p[0]), vs(p[1])]) for p in pairs}
    o = jnp.concatenate([jnp.concatenate([inter[(c, h)] + intra[(c, h)] for h in range(n_heads)], axis=1)
                         for c in range(n_chunks)], axis=0)
    return o, new_states


def _group_rmsnorm(o, g, n_heads):
    dv = o.shape[1] // n_heads
    parts = []
    for h in range(n_heads):
        oh = o[:, h * dv:(h + 1) * dv]
        parts.append(oh * lax.rsqrt(jnp.mean(oh * oh, axis=-1, keepdims=True) + NORM_EPS) * g)
    return jnp.concatenate(parts, axis=1)


SPLIT_CW = 256
PREP_MAX_ROWS = 512


def _split_gate_up(wgu_ref, wg_o, wu_o):
    n_rows, two_f = wgu_ref.shape
    half = SPLIT_CW // 2
    r = lax.broadcasted_iota(jnp.int32, (SPLIT_CW, SPLIT_CW), 0)
    c = lax.broadcasted_iota(jnp.int32, (SPLIT_CW, SPLIT_CW), 1)
    pick = jnp.logical_or(jnp.logical_and(c < half, r == 2 * c),
                          jnp.logical_and(c >= half, r == 2 * (c - half) + 1))
    sel = jnp.where(pick, 1.0, 0.0).astype(BF16)
    n_blocks = two_f // SPLIT_CW
    x = jnp.concatenate([wgu_ref[:, j * SPLIT_CW:(j + 1) * SPLIT_CW].astype(BF16) for j in range(n_blocks)],
                        axis=0)
    y = _dot(x, sel).astype(BF16)
    for j in range(n_blocks):
        wg_o[:, j * half:(j + 1) * half] = y[j * n_rows:(j + 1) * n_rows, :half]
        wu_o[:, j * half:(j + 1) * half] = y[j * n_rows:(j + 1) * n_rows, half:]


def _side_rows(total_rows, n_steps):
    rows = total_rows // n_steps
    ok = rows * n_steps == total_rows and 16 <= rows <= PREP_MAX_ROWS and rows % 16 == 0
    return rows if ok else None


def _add_gate_up_side(in_specs, out_specs, out_shape, operands, w_side, step, n_steps):
    wgu2, lo, hi = w_side
    two_f = wgu2.shape[1]
    rows = _side_rows(hi - lo, n_steps)
    first = lo // rows
    assert first * rows == lo
    in_specs.append(pl.BlockSpec((rows, two_f), lambda b, h, c: (first + step(b, h, c), 0)))
    for _ in range(2):
        out_specs.append(pl.BlockSpec((rows, two_f // 2), lambda b, h, c: (step(b, h, c), 0)))
        out_shape.append(jax.ShapeDtypeStruct((hi - lo, two_f // 2), BF16))
    operands.append(wgu2)


def _gla_kernel(q_ref, k_ref, v_ref, r_ref, ma_ref, a1_ref, wa2_ref, ba_ref, g_ref, *rest):
    if len(rest) == 5:
        wgu_ref, o_ref, wg_o, wu_o, s_ref = rest
    else:
        o_ref, s_ref = rest

    @pl.when(pl.program_id(2) == 0)
    def _():
        s_ref[...] = jnp.zeros_like(s_ref)

    z = _dot(a1_ref[...], wa2_ref[...]) + ba_ref[...]
    lg = _log_sigmoid(z) * (1.0 / GLA_TAU)
    q = q_ref[...].astype(F32) * (GLA_DK ** -0.5)
    k = k_ref[...].astype(F32)
    o, new_states = _scan_block(q, k, v_ref[...], lg, [s_ref[...]], 1)
    s_ref[...] = new_states[0]
    o = _group_rmsnorm(o, g_ref[...], 1)
    r = r_ref[...].astype(F32)
    gate = _sigmoid(ma_ref[...].astype(F32))
    o_ref[...] = (gate * (o * (r * _sigmoid(r)))).astype(o_ref.dtype)
    if len(rest) == 5:
        _split_gate_up(wgu_ref, wg_o, wu_o)


def _gla_branch(proj, wa2p, ba, gla_g, batch, seq, blk, w_side=None):
    nb = seq // blk
    row = lambda b, h, c: b * nb + c
    step = lambda b, h, c: (b * GLA_HEADS + h) * nb + c
    kb, vb = GLA_DK, GLA_DV
    in_specs = [
        pl.BlockSpec((blk, kb), lambda b, h, c: (row(b, h, c), OFF_GQ // kb + h)),
        pl.BlockSpec((blk, kb), lambda b, h, c: (row(b, h, c), OFF_GK // kb + h)),
        pl.BlockSpec((blk, vb), lambda b, h, c: (row(b, h, c), OFF_GV // vb + h)),
        pl.BlockSpec((blk, vb), lambda b, h, c: (row(b, h, c), OFF_GR // vb + h)),
        pl.BlockSpec((blk, vb), lambda b, h, c: (row(b, h, c), OFF_MA // vb + h)),
        pl.BlockSpec((blk, LANES), lambda b, h, c: (row(b, h, c), OFF_A1 // LANES)),
        pl.BlockSpec((LANES, kb), lambda b, h, c: (0, h)),
        pl.BlockSpec((1, kb), lambda b, h, c: (0, h)),
        pl.BlockSpec((1, vb), lambda b, h, c: (0, 0)),
    ]
    out_specs = [pl.BlockSpec((blk, vb), lambda b, h, c: (row(b, h, c), h))]
    out_shape = [jax.ShapeDtypeStruct((batch * seq, GLA_VAL), BF16)]
    operands = [proj, proj, proj, proj, proj, proj, wa2p, ba, gla_g]
    if w_side is not None:
        _add_gate_up_side(in_specs, out_specs, out_shape, operands, w_side, step, batch * GLA_HEADS * nb)
    return pl.pallas_call(
        _gla_kernel,
        grid=(batch, GLA_HEADS, nb),
        in_specs=in_specs,
        out_specs=out_specs,
        out_shape=out_shape,
        scratch_shapes=[pltpu.VMEM((vb, kb), F32)],
        compiler_params=_cparams(("arbitrary", "arbitrary", "arbitrary")),
        name="gla_scan",
    )(*operands)


HGRN_HP = 4


def _hgrn_kernel(q_ref, f_ref, i_ref, g_ref, mb_ref, gla_ref, lbl_ref, ng_ref, *rest):
    if len(rest) == 5:
        wgu_ref, o_ref, wg_o, wu_o, s_ref = rest
    else:
        o_ref, s_ref = rest

    @pl.when(pl.program_id(2) == 0)
    def _():
        s_ref[...] = jnp.zeros_like(s_ref)

    lbl = lbl_ref[...].astype(F32)
    e = jnp.exp(lbl - jnp.max(lbl, axis=0, keepdims=True))
    lb = e[0:1, :] / jnp.sum(e, axis=0, keepdims=True)

    hq = q_ref[...].astype(F32)
    hf = f_ref[...].astype(F32)
    q = hq * _sigmoid(hq)
    th = jnp.tanh(0.5 * hf)
    half = 0.5 * (1.0 - lb)
    lg = jnp.log(lb + half * (1.0 + th))
    k = half * (1.0 - th)
    o, new_states = _scan_block(q, k, i_ref[...], lg, [s_ref[h] for h in range(HGRN_HP)], HGRN_HP)
    for h in range(HGRN_HP):
        s_ref[h] = new_states[h]
    o = _group_rmsnorm(o, ng_ref[...], HGRN_HP)
    o = o * _sigmoid(g_ref[...].astype(F32))
    merged = gla_ref[...].astype(F32) + _sigmoid(mb_ref[...].astype(F32)) * o
    o_ref[...] = merged.astype(o_ref.dtype)
    if len(rest) == 5:
        _split_gate_up(wgu_ref, wg_o, wu_o)


def _hgrn_branch(proj, gla_out, lb_logits, hgrn_g, batch, seq, blk, w_side=None):
    nb = seq // blk
    w = HGRN_HP * HGRN_DK
    n_groups = HGRN_HEADS // HGRN_HP
    row = lambda b, h, c: b * nb + c
    step = lambda b, h, c: (b * n_groups + h) * nb + c
    in_specs = [
        pl.BlockSpec((blk, w), lambda b, h, c: (row(b, h, c), OFF_HQ // w + h)),
        pl.BlockSpec((blk, w), lambda b, h, c: (row(b, h, c), OFF_HF // w + h)),
        pl.BlockSpec((blk, w), lambda b, h, c: (row(b, h, c), OFF_HI // w + h)),
        pl.BlockSpec((blk, w), lambda b, h, c: (row(b, h, c), OFF_HG // w + h)),
        pl.BlockSpec((blk, w), lambda b, h, c: (row(b, h, c), OFF_MB // w + h)),
        pl.BlockSpec((blk, w), lambda b, h, c: (row(b, h, c), h)),
        pl.BlockSpec((DEPTH + 1, w), lambda b, h, c: (0, h)),
        pl.BlockSpec((1, HGRN_DV), lambda b, h, c: (0, 0)),
    ]
    out_specs = [pl.BlockSpec((blk, w), lambda b, h, c: (row(b, h, c), h))]
    out_shape = [jax.ShapeDtypeStruct((batch * seq, D_MODEL), BF16)]
    operands = [proj, proj, proj, proj, proj, gla_out, lb_logits, hgrn_g]
    if w_side is not None:
        _add_gate_up_side(in_specs, out_specs, out_shape, operands, w_side, step, batch * n_groups * nb)
    return pl.pallas_call(
        _hgrn_kernel,
        grid=(batch, n_groups, nb),
        in_specs=in_specs,
        out_specs=out_specs,
        out_shape=out_shape,
        scratch_shapes=[pltpu.VMEM((HGRN_HP, HGRN_DV, HGRN_DK), F32)],
        compiler_params=_cparams(("arbitrary", "arbitrary", "arbitrary")),
        name="hgrn_scan",
    )(*operands)


def _out_ln_kernel(m_ref, w_ref, x_ref, g_ref, b_ref, o_ref):
    y = _dot(m_ref[...], w_ref[...])
    o_ref[...] = _layer_norm(DEEPNORM_ALPHA * x_ref[...] + y, g_ref[...], b_ref[...])


def _resident(shape):
    return pl.BlockSpec(shape, lambda *_: (0,) * len(shape), pipeline_mode=pl.Buffered(1))


def _out_ln(merged, w_o, x2, g, b, tm):
    t = merged.shape[0]
    tm = min(tm, t)
    return pl.pallas_call(
        _out_ln_kernel,
        grid=(t // tm,),
        in_specs=[pl.BlockSpec((tm, D_MODEL), lambda i: (i, 0)),
                  _resident((D_MODEL, D_MODEL)),
                  pl.BlockSpec((tm, D_MODEL), lambda i: (i, 0)),
                  _resident((1, D_MODEL)), _resident((1, D_MODEL))],
        out_specs=pl.BlockSpec((tm, D_MODEL), lambda i: (i, 0)),
        out_shape=jax.ShapeDtypeStruct((t, D_MODEL), F32),
        compiler_params=_cparams(("arbitrary",)),
        name="wo_ln1",
    )(merged, w_o, x2, g, b)


def _xattn_kernel(h_ref, wq_ref, kv_ref, wo_ref, g_ref, b_ref, o_ref):
    h = h_ref[...]
    q = _dot(h.astype(BF16), wq_ref[...]).astype(BF16)
    outs = []
    for hh in range(X_HEADS):
        sl = slice(hh * X_DH, (hh + 1) * X_DH)
        kh = kv_ref[0, :, sl]
        vh = kv_ref[0, :, D_MODEL + hh * X_DH:D_MODEL + (hh + 1) * X_DH]
        s = _dot_nt(q[:, sl], kh) * (X_DH ** -0.5)
        p = jnp.exp(s - jnp.max(s, axis=-1, keepdims=True))
        l = jnp.sum(p, axis=-1, keepdims=True)
        outs.append((_dot(p.astype(BF16), vh) * (1.0 / l)).astype(BF16))
    o = jnp.concatenate(outs, axis=1)
    y = _dot(o, wo_ref[...])
    o_ref[...] = _layer_norm(DEEPNORM_ALPHA * h + y, g_ref[...], b_ref[...])


def _xattn(h1, w_xq, kv, w_xo, g, b, seq, tm):
    t = h1.shape[0]
    tm = min(tm, seq)
    per_b = seq // tm
    return pl.pallas_call(
        _xattn_kernel,
        grid=(t // tm,),
        in_specs=[pl.BlockSpec((tm, D_MODEL), lambda i: (i, 0)),
                  _resident((D_MODEL, D_MODEL)),
                  pl.BlockSpec((1, N_MEM, 2 * D_MODEL), lambda i: (i // per_b, 0, 0)),
                  _resident((D_MODEL, D_MODEL)),
                  _resident((1, D_MODEL)), _resident((1, D_MODEL))],
        out_specs=pl.BlockSpec((tm, D_MODEL), lambda i: (i, 0)),
        out_shape=jax.ShapeDtypeStruct((t, D_MODEL), F32),
        compiler_params=_cparams(("arbitrary",)),
        name="xattn_ln2",
    )(h1, w_xq, kv, w_xo, g, b)


def _router_kernel(h_ref, w_ref, b_ref, idx_ref, gate_ref):
    hh, hl = _split_bf16(h_ref[...])
    wh, wl = _split_bf16(w_ref[...])
    logits = _dot(hh, wh) + _dot(hh, wl) + _dot(hl, wh) + b_ref[...]
    lane = lax.broadcasted_iota(jnp.int32, logits.shape, 1)
    neg = jnp.float32(-jnp.inf)
    logits = jnp.where(lane < N_EXPERTS, logits, neg)
    idx_out = jnp.zeros(logits.shape, jnp.int32)
    tops = []
    for k in range(TOP_K):
        m = jnp.max(logits, axis=-1, keepdims=True)
        idx = jnp.min(jnp.where(logits == m, lane, LANES), axis=-1, keepdims=True)
        idx_out = jnp.where(lane == k, idx, idx_out)
        logits = jnp.where(lane == idx, neg, logits)
        tops.append(m)
    ex = [jnp.exp(m - tops[0]) for m in tops]
    den = ex[0] + ex[1] + ex[2] + ex[3]
    gate_out = jnp.zeros(logits.shape, F32)
    for k in range(TOP_K):
        gate_out = jnp.where(lane == k, ex[k] / den, gate_out)
    idx_ref[...] = idx_out
    gate_ref[...] = gate_out


def _router(h2, w_rp, b_rp, tm):
    t = h2.shape[0]
    tm = min(tm, t)
    return pl.pallas_call(
        _router_kernel,
        grid=(t // tm,),
        in_specs=[pl.BlockSpec((tm, D_MODEL), lambda i: (i, 0)),
                  _resident((D_MODEL, LANES)), _resident((1, LANES))],
        out_specs=[pl.BlockSpec((tm, LANES), lambda i: (i, 0)),
                   pl.BlockSpec((tm, LANES), lambda i: (i, 0))],
        out_shape=[jax.ShapeDtypeStruct((t, LANES), jnp.int32),
                   jax.ShapeDtypeStruct((t, LANES), F32)],
        compiler_params=_cparams(("arbitrary",)),
        name="router_top4",
    )(h2, w_rp, b_rp)


DISPATCH_TM = 256
ISSUE_UNROLL = 8
MOE_ROWS = 1024
MOE_SUB = 256
PAD_SLOTS = N_EXPERTS * MOE_SUB


def _dispatch_kernel(dest_ref, pad_ref, h_ref, xs_hbm, stage_ref, zero_ref, sem, zsem):
    i = pl.program_id(0)
    n_tiles = pl.num_programs(0) - 1
    tm = h_ref.shape[0]
    n = TOP_K * tm

    @pl.when(i < n_tiles)
    def _():
        slot = i % 2
        stage_ref[slot] = h_ref[...]

        def issue(j8, c):
            for u in range(ISSUE_UNROLL):
                j = j8 * ISSUE_UNROLL + u
                tok = j8 * (ISSUE_UNROLL // TOP_K) + u // TOP_K
                pltpu.make_async_copy(stage_ref.at[slot, pl.ds(tok, 1), :],
                                      xs_hbm.at[pl.ds(dest_ref[0, 0, j], 1), :],
                                      sem.at[slot]).start(priority=u % 2)
            return c
        lax.fori_loop(0, n // ISSUE_UNROLL, issue, 0)

    @pl.when(i >= 1)
    def _():
        for _ in range(TOP_K):
            pltpu.make_async_copy(stage_ref.at[0, pl.ds(0, tm), :], xs_hbm.at[pl.ds(0, tm), :],
                                  sem.at[(i + 1) % 2]).wait()

    @pl.when(i == n_tiles)
    def _():
        zero_ref[...] = jnp.zeros_like(zero_ref)

        def pad_copy(j):
            return pltpu.make_async_copy(zero_ref.at[pl.ds(0, 1), :],
                                         xs_hbm.at[pl.ds(pad_ref[0, 0, j], 1), :], zsem)

        def issue(j, c):
            @pl.when(pad_ref[0, 0, j] >= 0)
            def _():
                pad_copy(j).start()
            return c
        lax.fori_loop(0, PAD_SLOTS, issue, 0)

        def drain(j, c):
            @pl.when(pad_ref[0, 0, j] >= 0)
            def _():
                pad_copy(j).wait()
            return c
        lax.fori_loop(0, PAD_SLOTS, drain, 0)


def _dispatch(h2, dest_tiles, pad_dest, n_rows):
    t = h2.shape[0]
    tm = min(DISPATCH_TM, t)
    n_tiles = t // tm
    last = n_tiles - 1
    return pl.pallas_call(
        _dispatch_kernel,
        grid=(n_tiles + 1,),
        in_specs=[pl.BlockSpec((1, 1, TOP_K * tm), lambda i: (jnp.minimum(i, last), 0, 0),
                               memory_space=pltpu.SMEM),
                  pl.BlockSpec((1, 1, PAD_SLOTS), lambda i: (0, 0, 0), memory_space=pltpu.SMEM),
                  pl.BlockSpec((tm, D_MODEL), lambda i: (jnp.minimum(i, last), 0))],
        out_specs=pl.BlockSpec(memory_space=pl.ANY),
        out_shape=jax.ShapeDtypeStruct((n_rows, D_MODEL), F32),
        scratch_shapes=[pltpu.VMEM((2, tm, D_MODEL), F32), pltpu.VMEM((8, D_MODEL), F32),
                        pltpu.SemaphoreType.DMA((2,)), pltpu.SemaphoreType.DMA(())],
        compiler_params=_cparams(("arbitrary",)),
        name="moe_dispatch",
    )(dest_tiles, pad_dest, h2)


PREP_TK = 256


def _wprep_kernel(wgu_ref, wd_ref, wg_o, wu_o, wd_o):
    _split_gate_up(wgu_ref, wg_o, wu_o)
    wd_o[...] = wd_ref[...].astype(BF16)


def _wprep(wgu2, wd2):
    total, two_f = wgu2.shape
    spec = lambda width: pl.BlockSpec((PREP_TK, width), lambda i: (i, 0))
    assert wd2.shape[0] == total, "gate/up and down weights are walked with one row index"
    return pl.pallas_call(
        _wprep_kernel,
        grid=(total // PREP_TK,),
        in_specs=[spec(two_f), spec(wd2.shape[1])],
        out_specs=[spec(two_f // 2), spec(two_f // 2), spec(wd2.shape[1])],
        out_shape=[jax.ShapeDtypeStruct((total, two_f // 2), BF16),
                   jax.ShapeDtypeStruct((total, two_f // 2), BF16),
                   jax.ShapeDtypeStruct(wd2.shape, BF16)],
        compiler_params=_cparams(("arbitrary",)),
        name="moe_weight_prep",
    )(wgu2, wd2)


MOE_TF = 512
MOE_NF = D_EXPERT // MOE_TF
MOE_NN = D_MODEL // MOE_TF


def _moe_kernel(n_lo, be_ref, nv_ref, nu_ref, x_ref, wg_lo, wu_lo, wg_hi, wu_hi, bg_ref, bu_ref, wd_ref, bd_ref,
                o_ref, xb_ref, act_ref):
    blk = pl.program_id(0)
    s = pl.program_id(1)
    n_valid = nv_ref[blk]
    n_rows = x_ref.shape[0]
    n_sub = n_rows // MOE_SUB
    full = n_valid == n_rows
    low = be_ref[blk] < n_lo

    def cast_x(rows):
        xb_ref[rows, :] = x_ref[rows, :].astype(BF16)

    def gate_up(rows, wg_ref, wu_ref):
        xs = xb_ref[rows, :]
        glu = jnp.minimum(_dot(xs, wg_ref[0]) + bg_ref[0], SWIGLU_LIMIT)
        lin = jnp.clip(_dot(xs, wu_ref[0]) + bu_ref[0], -SWIGLU_LIMIT, SWIGLU_LIMIT)
        act_ref[s, rows, :] = (glu * _sigmoid(SWIGLU_ALPHA * glu) * (lin + 1.0)).astype(BF16)

    def down(rows):
        act = jnp.concatenate([act_ref[f, rows, :] for f in range(MOE_NF)], axis=1)
        o_ref[rows, :] = _dot(act, wd_ref[0]) + bd_ref[0]

    def phases(rows, active):
        pl.when(jnp.logical_and(active, s == 0))(lambda: cast_x(rows))
        up_step = jnp.logical_and(active, s < MOE_NF)
        pl.when(jnp.logical_and(up_step, low))(lambda: gate_up(rows, wg_lo, wu_lo))
        pl.when(jnp.logical_and(up_step, jnp.logical_not(low)))(lambda: gate_up(rows, wg_hi, wu_hi))
        pl.when(jnp.logical_and(active, s >= MOE_NF))(lambda: down(rows))

    phases(slice(0, n_rows), full)
    for t in range(n_sub):
        rows = slice(t * MOE_SUB, (t + 1) * MOE_SUB)
        phases(rows, jnp.logical_and(t * MOE_SUB < n_valid, jnp.logical_not(full)))

        @pl.when(jnp.logical_and(t * MOE_SUB >= n_valid, s >= MOE_NF))
        def _(rows=rows):
            o_ref[rows, :] = jnp.zeros((MOE_SUB, MOE_TF), F32)


def _moe_ffn(x_sorted, block_e, n_valid, n_used, wg_lo, wu_lo, wg_hi, wu_hi, bg3, bu3, wd, bd3):
    rows = MOE_ROWS
    n_blocks = x_sorted.shape[0] // rows
    n_lo = wg_lo.shape[0]
    up = lambda s: jnp.minimum(s, MOE_NF - 1)
    down = lambda s: jnp.maximum(s - MOE_NF, 0)
    lo_spec = pl.BlockSpec((1, D_MODEL, MOE_TF), lambda b, s, be, nv, nu: (
        jnp.minimum(be[b], n_lo - 1), 0, jnp.where(be[b] < n_lo, up(s), MOE_NF - 1)))
    hi_spec = pl.BlockSpec((1, D_MODEL, MOE_TF), lambda b, s, be, nv, nu: (
        jnp.maximum(be[b] - n_lo, 0), 0, jnp.where(be[b] >= n_lo, up(s), 0)))
    grid_spec = pltpu.PrefetchScalarGridSpec(
        num_scalar_prefetch=3,
        grid=(n_blocks, MOE_NF + MOE_NN),
        in_specs=[
            pl.BlockSpec((rows, D_MODEL), lambda b, s, be, nv, nu: (jnp.minimum(b, nu[0] - 1), 0)),
            lo_spec, lo_spec, hi_spec, hi_spec,
            pl.BlockSpec((1, 1, MOE_TF), lambda b, s, be, nv, nu: (be[b], 0, up(s))),
            pl.BlockSpec((1, 1, MOE_TF), lambda b, s, be, nv, nu: (be[b], 0, up(s))),
            pl.BlockSpec((1, D_EXPERT, MOE_TF), lambda b, s, be, nv, nu: (be[b], 0, down(s))),
            pl.BlockSpec((1, 1, MOE_TF), lambda b, s, be, nv, nu: (be[b], 0, down(s))),
        ],
        out_specs=pl.BlockSpec((rows, MOE_TF), lambda b, s, be, nv, nu: (b, down(s))),
        scratch_shapes=[pltpu.VMEM((rows, D_MODEL), BF16),
                        pltpu.VMEM((MOE_NF, rows, MOE_TF), BF16)],
    )
    return pl.pallas_call(
        functools.partial(_moe_kernel, n_lo),
        grid_spec=grid_spec,
        out_shape=jax.ShapeDtypeStruct((n_blocks * rows, D_MODEL), F32),
        compiler_params=_cparams(("arbitrary", "arbitrary")),
        name="moe_ffn",
    )(block_e, n_valid, n_used, x_sorted, wg_lo, wu_lo, wg_hi, wu_hi, bg3, bu3, wd, bd3)


COMBINE_TM = 256


def _combine_kernel(dcur_ref, dnxt_ref, y_hbm, h_ref, gate_ref, g_ref, b_ref, o_ref, yb_ref, sem):
    i = pl.program_id(0)
    n_tiles = pl.num_programs(0)
    tm = h_ref.shape[0]
    n = TOP_K * tm

    def issue_tile(d_ref, slot):
        def issue(j8, c):
            for u in range(ISSUE_UNROLL):
                j = j8 * ISSUE_UNROLL + u
                pltpu.make_async_copy(y_hbm.at[pl.ds(d_ref[0, 0, j], 1), :],
                                      yb_ref.at[slot, pl.ds(j, 1), :], sem.at[slot]).start(priority=u % 2)
            return c
        lax.fori_loop(0, n // ISSUE_UNROLL, issue, 0)

    @pl.when(i == 0)
    def _():
        issue_tile(dcur_ref, 0)

    @pl.when(i + 1 < n_tiles)
    def _():
        issue_tile(dnxt_ref, (i + 1) % 2)

    slot = i % 2
    pltpu.make_async_copy(y_hbm.at[pl.ds(0, n), :], yb_ref.at[slot], sem.at[slot]).wait()
    gate = gate_ref[...]
    acc = DEEPNORM_ALPHA * h_ref[...]
    for k in range(TOP_K):
        acc = acc + gate[:, k:k + 1] * yb_ref[slot, k * tm:(k + 1) * tm, :]
    o_ref[...] = _layer_norm(acc, g_ref[...], b_ref[...])


def _combine(dest3, y_sorted, h2, gates, g, b, tm):
    t = h2.shape[0]
    n_tiles = t // tm
    return pl.pallas_call(
        _combine_kernel,
        grid=(n_tiles,),
        in_specs=[pl.BlockSpec((1, 1, TOP_K * tm), lambda i: (i, 0, 0), memory_space=pltpu.SMEM),
                  pl.BlockSpec((1, 1, TOP_K * tm), lambda i: (jnp.minimum(i + 1, n_tiles - 1), 0, 0),
                               memory_space=pltpu.SMEM),
                  pl.BlockSpec(memory_space=pl.ANY),
                  pl.BlockSpec((tm, D_MODEL), lambda i: (i, 0)),
                  pl.BlockSpec((tm, LANES), lambda i: (i, 0)),
                  _resident((1, D_MODEL)), _resident((1, D_MODEL))],
        out_specs=pl.BlockSpec((tm, D_MODEL), lambda i: (i, 0)),
        out_shape=jax.ShapeDtypeStruct((t, D_MODEL), F32),
        scratch_shapes=[pltpu.VMEM((2, TOP_K * tm, D_MODEL), F32), pltpu.SemaphoreType.DMA((2,))],
        compiler_params=_cparams(("arbitrary",)),
        name="moe_combine_ln3",
    )(dest3, dest3, y_sorted, h2, gates, g, b)


def _dispatch_plan(top_e, n_tok):
    rows = MOE_ROWS
    n_assign = n_tok * TOP_K
    e_flat = top_e.reshape(n_assign)
    onehot = (e_flat[:, None] == jnp.arange(N_EXPERTS, dtype=jnp.int32)[None, :]).astype(jnp.int32)
    csum = jnp.cumsum(onehot, axis=0)
    pos = jnp.sum(csum * onehot, axis=1) - 1
    counts = csum[-1]
    padded = (counts + rows - 1) // rows * rows
    pend = jnp.cumsum(padded)
    pstart = pend - padded
    dest = (pstart[e_flat] + pos).astype(jnp.int32)
    n_blocks = -(-n_assign // rows) + N_EXPERTS
    n_used = (pend[-1] // rows).astype(jnp.int32)
    blk_idx = jnp.arange(n_blocks, dtype=jnp.int32)
    blk_start = blk_idx * rows
    block_e = jnp.minimum(jnp.sum((pend[None, :] <= blk_start[:, None]).astype(jnp.int32), axis=1),
                          N_EXPERTS - 1)
    n_valid = jnp.clip(counts[block_e] - (blk_start - pstart[block_e]), 0, rows)
    n_valid = jnp.where(blk_idx < n_used, n_valid, 0).astype(jnp.int32)
    block_e = jnp.where(blk_idx < n_used, block_e, block_e[n_used - 1]).astype(jnp.int32)
    j = jnp.arange(MOE_SUB, dtype=jnp.int32)[None, :]
    sub_end = (counts + MOE_SUB - 1) // MOE_SUB * MOE_SUB
    pad_dest = jnp.where(counts[:, None] + j < sub_end[:, None], (pstart + counts)[:, None] + j, -1)
    pad_dest = pad_dest.astype(jnp.int32).reshape(1, 1, PAD_SLOTS)
    return dest, pad_dest, block_e, n_valid, n_used.reshape(1), n_blocks * rows


def kernel(x, mem, w_in, b_in, w_gla_a2, b_gla_a, gla_norm_g, hgrn_norm_g, hgrn_lb_logits, w_mix_o,
           w_xq, w_mem_kv, w_xo, w_router, b_router, w_gate_up, b_gate_up, w_down, b_down,
           ln1_g, ln1_b, ln2_g, ln2_b, ln3_g, ln3_b):
    batch, seq, d = x.shape
    t = batch * seq
    x2 = x.reshape(t, d)
    l = 0

    pad = D_IN_PAD - OFF_A1 - GLA_RANK
    w = w_in[l]
    w_gla = w[:, :GA1_SRC].astype(BF16)
    w_hgrn = w[:, GA1_SRC + GLA_RANK:].astype(BF16)
    w_a1 = jnp.pad(w[:, GA1_SRC:GA1_SRC + GLA_RANK].astype(BF16), ((0, 0), (0, LANES - GLA_RANK)))
    bi = b_in[l]
    b_in_p = jnp.concatenate([bi[:GA1_SRC], bi[GA1_SRC + GLA_RANK:], bi[GA1_SRC:GA1_SRC + GLA_RANK],
                              jnp.zeros((pad,), bi.dtype)])[None, :]
    n_e, _, two_f = w_gate_up[l].shape
    wgu2 = w_gate_up[l].reshape(n_e * d, two_f)
    wd2 = w_down[l].reshape(n_e * D_EXPERT, d)
    split = (n_e // 2) * d
    blk = min(512, seq)
    nb = seq // blk
    tm_p = min(1024, t)
    fused_prep = (_side_rows(split, batch * (HGRN_HEADS // HGRN_HP) * nb) is not None
                  and _side_rows(wgu2.shape[0] - split, batch * GLA_HEADS * nb) is not None
                  and wd2.shape[0] % PROJ_SIDE_ROWS == 0
                  and wd2.shape[0] // PROJ_SIDE_ROWS <= (t // tm_p) * (D_IN_PAD // PROJ_TN))
    proj_res = _in_proj(x2, w_gla, w_hgrn, w_a1, b_in_p, tm_p, wd2 if fused_prep else None)
    proj = proj_res[0]

    wa2p = jnp.concatenate([w_gla_a2[l], jnp.zeros((LANES - GLA_RANK, GLA_KEY), F32)], axis=0).astype(BF16)
    gla_res = _gla_branch(proj, wa2p, b_gla_a[l][None, :], gla_norm_g[l][None, :], batch, seq, blk,
                          (wgu2, split, wgu2.shape[0]) if fused_prep else None)
    hgrn_res = _hgrn_branch(proj, gla_res[0], hgrn_lb_logits, hgrn_norm_g[l][None, :], batch, seq, blk,
                            (wgu2, 0, split) if fused_prep else None)
    merged = hgrn_res[0]
    if fused_prep:
        wd_b = proj_res[1]
        wg_lo, wu_lo = hgrn_res[1:]
        wg_hi, wu_hi = gla_res[1:]
    else:
        wg_b, wu_b, wd_b = _wprep(wgu2, wd2)
        wg_lo, wg_hi, wu_lo, wu_hi = wg_b[:split], wg_b[split:], wu_b[:split], wu_b[split:]

    h1 = _out_ln(merged, w_mix_o[l].astype(BF16), x2, ln1_g[l][None, :], ln1_b[l][None, :], 512)

    kv = _matmul_bias(mem.reshape(batch * N_MEM, d), w_mem_kv[l].astype(BF16),
                      jnp.zeros((1, 2 * d), F32), 512, 512, BF16).reshape(batch, N_MEM, 2 * d)
    h2 = _xattn(h1, w_xq[l].astype(BF16), kv, w_xo[l].astype(BF16),
                ln2_g[l][None, :], ln2_b[l][None, :], seq, 512)

    w_rp = jnp.concatenate([w_router[l], jnp.zeros((d, LANES - N_EXPERTS), F32)], axis=1)
    b_rp = jnp.concatenate([b_router[l], jnp.zeros((LANES - N_EXPERTS,), F32)])[None, :]
    idx, gates = _router(h2, w_rp, b_rp, 512)

    dest, pad_dest, block_e, n_valid, n_used, n_rows = _dispatch_plan(idx[:, :TOP_K], t)
    tm_d = min(DISPATCH_TM, t)
    x_sorted = _dispatch(h2, dest.reshape(t // tm_d, 1, TOP_K * tm_d), pad_dest, n_rows)
    per_expert = lambda a: a.reshape(-1, d, two_f // 2)
    y_sorted = _moe_ffn(x_sorted, block_e, n_valid, n_used,
                        per_expert(wg_lo), per_expert(wu_lo), per_expert(wg_hi), per_expert(wu_hi),
                        b_gate_up[l][:, None, 0::2], b_gate_up[l][:, None, 1::2],
                        wd_b.reshape(n_e, D_EXPERT, d), b_down[l][:, None, :])

    tm_c = min(COMBINE_TM, t)
    dest3 = dest.reshape(t // tm_c, tm_c, TOP_K).transpose(0, 2, 1).reshape(t // tm_c, 1, TOP_K * tm_c)
    out = _combine(dest3, y_sorted, h2, gates, ln3_g[l][None, :], ln3_b[l][None, :], tm_c)
    return out.reshape(batch, seq, d)
```
